```python
import jax, jax.numpy as jnp
from jax import lax
import numpy as np

D_MODEL = 2048
BATCH = 1
SEQ = 8192
DEPTH = 1
DEC_BATCH = 128
DEC_SEQ = 8
PAST_LEN = 2048
PAGE_SIZE = 128

N_HEADS = 8
HEAD_DIM = 128
ATTN_W = N_HEADS * HEAD_DIM
C_CONV = D_MODEL - ATTN_W
MIX_W = ATTN_W + C_CONV
IN_COLS = 3 * ATTN_W + 2 * C_CONV
CONV_W = 31
MOBA_BLOCK = 256
MOBA_TOP_K = 3
ROPE_THETA = 10000.0
Q_CHUNK = 64
PLE_DIM = 256
PEER_HEADS = 8
PEER_KEYS = 128
N_EXPERTS = PEER_KEYS * PEER_KEYS
PEER_TOPK = 16
PEER_QDIM = 256
PEER_HALF = PEER_QDIM // 2
PEER_CHUNK = 128
ALPHA = (2.0 * DEPTH) ** 0.25
BETA = (8.0 * DEPTH) ** -0.25
LN_EPS = 1e-5

kernel_name = 'hybrid_moba_conformer_peer_step'


def layer_norm(x, g, b):
    xf = x.astype(jnp.float32)
    mu = jnp.mean(xf, axis=-1, keepdims=True)
    var = jnp.mean(jnp.square(xf - mu), axis=-1, keepdims=True)
    y = (xf - mu) * lax.rsqrt(var + LN_EPS) * g.astype(jnp.float32) + b.astype(jnp.float32)
    return y.astype(x.dtype)


def rotary(x, pos):
    half = HEAD_DIM // 2
    inv = ROPE_THETA ** (-jnp.arange(half, dtype=jnp.float32) * 2.0 / HEAD_DIM)
    ang = pos.astype(jnp.float32)[:, None] * inv[None, :]
    cos = jnp.cos(ang)[:, None, :]
    sin = jnp.sin(ang)[:, None, :]
    xf = x.astype(jnp.float32)
    x1, x2 = xf[..., :half], xf[..., half:]
    return jnp.concatenate([x1 * cos - x2 * sin, x2 * cos + x1 * sin], axis=-1).astype(x.dtype)


def project(x, pos, w_in):
    b, s, _ = x.shape
    z = x @ w_in
    q = z[..., :ATTN_W].reshape(b, s, N_HEADS, HEAD_DIM)
    k = z[..., ATTN_W:2 * ATTN_W].reshape(b, s, N_HEADS, HEAD_DIM)
    v = z[..., 2 * ATTN_W:3 * ATTN_W].reshape(b, s, N_HEADS, HEAD_DIM)
    a = z[..., 3 * ATTN_W:3 * ATTN_W + C_CONV]
    gt = z[..., 3 * ATTN_W + C_CONV:]
    u = a * jax.nn.sigmoid(gt)
    return rotary(q, pos), rotary(k, pos), v, u


def moba_seq(q, q_pos, k, v, q_chunk):
    n_q = q.shape[0]
    seq_len = k.shape[0]
    nb = -(-seq_len // MOBA_BLOCK)
    pad = nb * MOBA_BLOCK - seq_len
    k = jnp.pad(k, ((0, pad), (0, 0), (0, 0)))
    v = jnp.pad(v, ((0, pad), (0, 0), (0, 0)))
    kb = k.reshape(nb, MOBA_BLOCK, N_HEADS, HEAD_DIM).transpose(2, 0, 1, 3)
    vb = v.reshape(nb, MOBA_BLOCK, N_HEADS, HEAD_DIM).transpose(2, 0, 1, 3)
    k_mean = jnp.mean(kb.astype(jnp.float32), axis=2)
    n_sel = min(MOBA_TOP_K, nb)
    scale = HEAD_DIM ** -0.5
    h_idx = jnp.arange(N_HEADS, dtype=jnp.int32)[None, :, None]
    offs = jnp.arange(MOBA_BLOCK, dtype=jnp.int32)

    def attend(args):
        qc, pc = args
        qcn = qc.shape[0]
        own = pc // MOBA_BLOCK
        gate = jnp.einsum('qhd,hnd->qhn', qc.astype(jnp.float32), k_mean)
        past = jnp.arange(nb, dtype=jnp.int32)[None, None, :] < own[:, None, None]
        gate = jnp.where(past, gate, -jnp.inf)
        _, sel = lax.top_k(gate, n_sel)
        valid = sel < own[:, None, None]
        own_b = jnp.broadcast_to(own[:, None, None], (qcn, N_HEADS, 1)).astype(sel.dtype)
        blocks = jnp.concatenate([sel, own_b], axis=-1)
        valid = jnp.concatenate([valid, jnp.ones((qcn, N_HEADS, 1), bool)], axis=-1)
        k_sel = kb[h_idx, blocks]
        v_sel = vb[h_idx, blocks]
        logits = jnp.einsum('qhd,qhsbd->qhsb', qc, k_sel).astype(jnp.float32) * scale
        kpos = blocks[..., None] * MOBA_BLOCK + offs
        allowed = valid[..., None] & (kpos <= pc[:, None, None, None])
        logits = jnp.where(allowed, logits, -jnp.inf)
        probs = jax.nn.softmax(logits.reshape(qcn, N_HEADS, -1), axis=-1).reshape(logits.shape)
        return jnp.einsum('qhsb,qhsbd->qhd', probs.astype(v_sel.dtype), v_sel)

    nqc = n_q // q_chunk
    out = lax.map(attend, (q.reshape(nqc, q_chunk, N_HEADS, HEAD_DIM), q_pos.reshape(nqc, q_chunk)))
    return out.reshape(n_q, N_HEADS, HEAD_DIM)


def conv_module(u_hist, conv_w, conv_b, ln_g, ln_b):
    y = lax.conv_general_dilated(
        u_hist, conv_w[:, None, :].astype(u_hist.dtype), window_strides=(1,), padding='VALID',
        dimension_numbers=('NWC', 'WIO', 'NWC'), feature_group_count=C_CONV)
    y = y + conv_b
    return jax.nn.silu(layer_norm(y, ln_g, ln_b))


def peer(h, w_q, sub_keys, u_tab, v_tab):
    b, s, d = h.shape
    n = b * s
    n_blocks = -(-n // PEER_CHUNK)
    xf = jnp.pad(h.reshape(n, d), ((0, n_blocks * PEER_CHUNK - n), (0, 0)))

    def block(xc):
        q = (xc @ w_q).astype(jnp.float32).reshape(PEER_CHUNK, PEER_HEADS, 2, PEER_HALF)
        sc = jnp.einsum('chpd,hpkd->chpk', q, sub_keys.astype(jnp.float32))
        s1, i1 = lax.top_k(sc[:, :, 0], PEER_TOPK)
        s2, i2 = lax.top_k(sc[:, :, 1], PEER_TOPK)
        cand_s = (s1[..., :, None] + s2[..., None, :]).reshape(PEER_CHUNK, PEER_HEADS, PEER_TOPK * PEER_TOPK)
        cand_i = (i1[..., :, None] * PEER_KEYS + i2[..., None, :]).reshape(PEER_CHUNK, PEER_HEADS, PEER_TOPK * PEER_TOPK)
        best_s, best_pos = lax.top_k(cand_s, PEER_TOPK)
        experts = jnp.take_along_axis(cand_i, best_pos, axis=-1)
        g = jax.nn.softmax(best_s, axis=-1)
        u = u_tab[experts]
        act = jax.nn.gelu(jnp.einsum('cd,chkd->chk', xc, u).astype(jnp.float32), approximate=False)
        wgt = (g * act).astype(xc.dtype)
        return jnp.einsum('chk,chkd->cd', wgt, v_tab[experts])

    out = lax.map(block, xf.reshape(n_blocks, PEER_CHUNK, d))
    return out.reshape(n_blocks * PEER_CHUNK, d)[:n].reshape(b, s, d)


def layer_block(x, p, attn, u_hist, w_out, conv_w, conv_b, conv_ln_g, conv_ln_b, ln1_g, ln1_b,
                peer_wq, peer_keys, peer_u, peer_v, ln2_g, ln2_b, ple_w, gate_w, gate_b):
    b, s, _ = x.shape
    conv_out = conv_module(u_hist, conv_w, conv_b, conv_ln_g, conv_ln_b)
    mixed = jnp.concatenate([attn.reshape(b, s, ATTN_W), conv_out.astype(attn.dtype)], axis=-1) @ w_out
    h = layer_norm(ALPHA * x + mixed, ln1_g, ln1_b)
    h = layer_norm(ALPHA * h + peer(h, peer_wq, peer_keys, peer_u, peer_v), ln2_g, ln2_b)
    gate = jax.nn.sigmoid(h @ gate_w + gate_b)
    return h + gate * (p @ ple_w)


def setup_inputs(seed: int = 0) -> dict:
    key = jax.random.key(seed)
    ks = jax.random.split(key, 32)
    f32 = jnp.float32
    n_pages = PAST_LEN // PAGE_SIZE
    n_used = DEC_BATCH * n_pages
    n_phys = n_used + max(1, n_used // 4)

    def nrm(k, shape, scale):
        return jax.random.normal(k, shape, f32) * scale

    x_prompt = nrm(ks[0], (BATCH, SEQ, D_MODEL), 1.0)
    x_sample = nrm(ks[1], (DEC_BATCH, DEC_SEQ, D_MODEL), 1.0)
    cache_k = nrm(ks[2], (DEPTH, n_phys, PAGE_SIZE, N_HEADS, HEAD_DIM), 1.0)
    cache_v = nrm(ks[3], (DEPTH, n_phys, PAGE_SIZE, N_HEADS, HEAD_DIM), 1.0)
    state_conv = nrm(ks[4], (DEPTH, DEC_BATCH, CONV_W - 1, C_CONV), 0.5)
    page_table = jax.random.permutation(ks[5], n_phys)[:n_used].reshape(DEC_BATCH, n_pages).astype(jnp.int32)
    p_prompt = nrm(ks[6], (DEPTH, BATCH, SEQ, PLE_DIM), 1.0)
    p_sample = nrm(ks[7], (DEPTH, DEC_BATCH, DEC_SEQ, PLE_DIM), 1.0)
    w_in = nrm(ks[8], (DEPTH, D_MODEL, IN_COLS), D_MODEL ** -0.5)
    w_in = w_in.at[:, :, 2 * ATTN_W:3 * ATTN_W].multiply(BETA)
    w_out = nrm(ks[9], (DEPTH, MIX_W, D_MODEL), BETA * MIX_W ** -0.5)
    conv_w = nrm(ks[10], (DEPTH, CONV_W, C_CONV), CONV_W ** -0.5)
    conv_b = nrm(ks[11], (DEPTH, C_CONV), 0.02)
    conv_ln_g = 1.0 + nrm(ks[12], (DEPTH, C_CONV), 0.02)
    conv_ln_b = nrm(ks[13], (DEPTH, C_CONV), 0.02)
    ln1_g = 1.0 + nrm(ks[14], (DEPTH, D_MODEL), 0.02)
    ln1_b = nrm(ks[15], (DEPTH, D_MODEL), 0.02)
    peer_wq = nrm(ks[16], (DEPTH, D_MODEL, PEER_HEADS * PEER_QDIM), D_MODEL ** -0.5)
    peer_keys = nrm(ks[17], (DEPTH, PEER_HEADS, 2, PEER_KEYS, PEER_HALF), PEER_HALF ** -0.5)
    peer_u = nrm(ks[18], (DEPTH, N_EXPERTS, D_MODEL), D_MODEL ** -0.5)
    peer_v = nrm(ks[19], (DEPTH, N_EXPERTS, D_MODEL), BETA)
    ln2_g = 1.0 + nrm(ks[20], (DEPTH, D_MODEL), 0.02)
    ln2_b = nrm(ks[21], (DEPTH, D_MODEL), 0.02)
    ple_w = nrm(ks[22], (DEPTH, PLE_DIM, D_MODEL), PLE_DIM ** -0.5)
    gate_w = nrm(ks[23], (DEPTH, D_MODEL, D_MODEL), D_MODEL ** -0.5)
    gate_b = nrm(ks[24], (DEPTH, D_MODEL), 0.02)
    return {'x_prompt': x_prompt, 'x_sample': x_sample, 'cache_k': cache_k, 'cache_v': cache_v,
            'state_conv': state_conv, 'page_table': page_table, 'p_prompt': p_prompt, 'p_sample': p_sample,
            'w_in': w_in, 'w_out': w_out, 'conv_w': conv_w, 'conv_b': conv_b, 'conv_ln_g': conv_ln_g,
            'conv_ln_b': conv_ln_b, 'ln1_g': ln1_g, 'ln1_b': ln1_b, 'peer_wq': peer_wq, 'peer_keys': peer_keys,
            'peer_u': peer_u, 'peer_v': peer_v, 'ln2_g': ln2_g, 'ln2_b': ln2_b, 'ple_w': ple_w,
            'gate_w': gate_w, 'gate_b': gate_b}


def reference(x_prompt, x_sample, cache_k, cache_v, state_conv, page_table, p_prompt, p_sample,
              w_in, w_out, conv_w, conv_b, conv_ln_g, conv_ln_b, ln1_g, ln1_b, peer_wq, peer_keys,
              peer_u, peer_v, ln2_g, ln2_b, ple_w, gate_w, gate_b):
    pos_p = jnp.arange(SEQ, dtype=jnp.int32)
    pos_s = PAST_LEN + jnp.arange(DEC_SEQ, dtype=jnp.int32)
    xp, xs = x_prompt, x_sample
    kp_l, vp_l, cp_l, ks_l, vs_l, cs_l = [], [], [], [], [], []
    for i in range(DEPTH):
        lw = (w_out[i], conv_w[i], conv_b[i], conv_ln_g[i], conv_ln_b[i], ln1_g[i], ln1_b[i],
              peer_wq[i], peer_keys[i], peer_u[i], peer_v[i], ln2_g[i], ln2_b[i], ple_w[i], gate_w[i], gate_b[i])
        qp, kp, vp, up = project(xp, pos_p, w_in[i])

        def prompt_seq(args):
            q_b, k_b, v_b = args
            return moba_seq(q_b, pos_p, k_b, v_b, Q_CHUNK)

        attn_p = lax.map(prompt_seq, (qp, kp, vp))
        hist_p = jnp.concatenate([jnp.zeros((xp.shape[0], CONV_W - 1, C_CONV), up.dtype), up], axis=1)
        xp_next = layer_block(xp, p_prompt[i], attn_p, hist_p, *lw)
        qs, kss, vss, us = project(xs, pos_s, w_in[i])
        k_pool, v_pool = cache_k[i], cache_v[i]

        def sample_seq(args, k_pool=k_pool, v_pool=v_pool):
            q_b, k_b, v_b, pt_b = args
            k_full = jnp.concatenate([k_pool[pt_b].reshape(-1, N_HEADS, HEAD_DIM), k_b.astype(k_pool.dtype)], axis=0)
            v_full = jnp.concatenate([v_pool[pt_b].reshape(-1, N_HEADS, HEAD_DIM), v_b.astype(v_pool.dtype)], axis=0)
            return moba_seq(q_b, pos_s, k_full, v_full, q_b.shape[0])

        attn_s = lax.map(sample_seq, (qs, kss, vss, page_table))
        hist_s = jnp.concatenate([state_conv[i].astype(us.dtype), us], axis=1)
        xs = layer_block(xs, p_sample[i], attn_s, hist_s, *lw)
        xp = xp_next
        kp_l.append(kp)
        vp_l.append(vp)
        cp_l.append(hist_p[:, -(CONV_W - 1):])
        ks_l.append(kss)
        vs_l.append(vss)
        cs_l.append(hist_s[:, -(CONV_W - 1):])
    return (xp, xs, jnp.stack(kp_l), jnp.stack(vp_l), jnp.stack(cp_l), jnp.stack(ks_l), jnp.stack(vs_l), jnp.stack(cs_l))
```

```python
import functools
import math

import jax
import jax.numpy as jnp
from jax import lax
from jax.experimental import pallas as pl
from jax.experimental.pallas import tpu as pltpu

N_HEADS = 8
HEAD_DIM = 128
ATTN_W = N_HEADS * HEAD_DIM
CONV_W = 31
MOBA_BLOCK = 256
MOBA_TOP_K = 3
ROPE_THETA = 10000.0
PAGE_SIZE = 128
PEER_HEADS = 8
PEER_KEYS = 128
PEER_TOPK = 16
PEER_HALF = 128
LN_EPS = 1e-5

LANES = 128
SUBLANES = 8
VMEM_LIMIT = 56 * 1024 * 1024

NEG = -1e30
F32 = jnp.float32
BF16 = jnp.bfloat16
NT_DIMS = (((1,), (1,)), ((), ()))


def _nt_dot(a, b, precision=None):
    return lax.dot_general(a, b, NT_DIMS, precision=precision, preferred_element_type=F32)


def _layer_norm(x, g, b):
    mu = jnp.mean(x, axis=-1, keepdims=True)
    xc = x - mu
    var = jnp.mean(xc * xc, axis=-1, keepdims=True)
    return xc * lax.rsqrt(var + LN_EPS) * g + b


def _params(*sem):
    return pltpu.CompilerParams(dimension_semantics=sem, vmem_limit_bytes=VMEM_LIMIT)


def _tile(n, t):
    t = min(n, t)
    assert n % t == 0, (n, t)
    return t


def _proj_kernel(x_ref, w_ref, cos_ref, sin_ref, q_ref, k_ref, v_ref, u_ref, xb_ref, a_ref):
    j = pl.program_id(1)

    @pl.when(j == 0)
    def _():
        xb_ref[...] = x_ref[...].astype(BF16)

    z = jnp.dot(xb_ref[...], w_ref[...], preferred_element_type=F32)

    def rotary_to(o_ref):
        c = cos_ref[...]
        s = sin_ref[...]
        for h in range(N_HEADS):
            zh = z[:, h * HEAD_DIM:(h + 1) * HEAD_DIM]
            o_ref[:, h * HEAD_DIM:(h + 1) * HEAD_DIM] = zh * c + pltpu.roll(zh, HEAD_DIM // 2, 1) * s

    @pl.when(j == 0)
    def _():
        rotary_to(q_ref)

    @pl.when(j == 1)
    def _():
        rotary_to(k_ref)

    @pl.when(j == 2)
    def _():
        v_ref[...] = z

    @pl.when(j == 3)
    def _():
        a_ref[...] = z

    @pl.when(j == 4)
    def _():
        u_ref[...] = a_ref[...] * jax.nn.sigmoid(z)


def _proj(x, w_in_b, cos_t, sin_t):
    n, d = x.shape
    tm = _tile(n, 512)
    row = lambda i, j: (i, 0)
    out = jax.ShapeDtypeStruct((n, ATTN_W), F32)
    return pl.pallas_call(
        _proj_kernel,
        grid=(n // tm, 5),
        in_specs=[pl.BlockSpec((tm, d), row),
                  pl.BlockSpec((d, ATTN_W), lambda i, j: (0, j)),
                  pl.BlockSpec((tm, HEAD_DIM), row),
                  pl.BlockSpec((tm, HEAD_DIM), row)],
        out_specs=[pl.BlockSpec((tm, ATTN_W), row)] * 4,
        out_shape=[out] * 4,
        scratch_shapes=[pltpu.VMEM((tm, d), BF16), pltpu.VMEM((tm, ATTN_W), F32)],
        compiler_params=_params("arbitrary", "arbitrary"),
        name="proj",
    )(x, w_in_b, cos_t, sin_t)


def _rotary_tables(pos):
    half = HEAD_DIM // 2
    inv = ROPE_THETA ** (-jnp.arange(half, dtype=F32) * 2.0 / HEAD_DIM)
    ang = pos.astype(F32)[:, None] * inv[None, :]
    c, s = jnp.cos(ang), jnp.sin(ang)
    return jnp.concatenate([c, c], axis=1), jnp.concatenate([-s, s], axis=1)


def _attn_prompt_kernel(q_ref, k_ref, v_ref, o_ref,
                        kaug_ref, vb_ref, kmean_ref, qaug_ref, m_ref, l_ref, acc_ref):
    i = pl.program_id(1)
    blk = MOBA_BLOCK
    nb = k_ref.shape[0] // blk
    scale = HEAD_DIM ** -0.5

    @pl.when(i == 0)
    def _():
        kmean_ref[...] = jnp.zeros_like(kmean_ref)

        def fill(j, carry):
            r = pl.multiple_of(j * blk, blk)
            kb = k_ref[pl.ds(r, blk), :]
            kmean_ref[pl.ds(j, 1), :] = jnp.sum(kb, axis=0, keepdims=True) * (1.0 / blk)
            hot = lax.broadcasted_iota(jnp.int32, (blk, LANES), 1) == j
            kaug_ref[pl.ds(r, blk), 0:HEAD_DIM] = kb.astype(BF16)
            kaug_ref[pl.ds(r, blk), HEAD_DIM:HEAD_DIM + LANES] = hot.astype(F32).astype(BF16)
            vb_ref[pl.ds(r, blk), :] = v_ref[pl.ds(r, blk), :].astype(BF16)
            return carry

        lax.fori_loop(0, nb, fill, 0)

    q = q_ref[...]
    gate = _nt_dot(q, kmean_ref[...], precision=lax.Precision.HIGHEST)
    col = lax.broadcasted_iota(jnp.int32, gate.shape, 1)
    past = col < i
    work = jnp.where(past, gate, -jnp.inf)
    sel = jnp.zeros(gate.shape, jnp.bool_)
    for _ in range(MOBA_TOP_K):
        mx = jnp.max(work, axis=1, keepdims=True)
        idx = jnp.min(jnp.where(work == mx, col, jnp.int32(1 << 30)), axis=1, keepdims=True)
        pick = col == idx
        sel = jnp.logical_or(sel, pick)
        work = jnp.where(pick, -jnp.inf, work)
    keep = jnp.logical_or(jnp.logical_and(sel, past), col == i)
    pen = jnp.where(keep, 0.0, NEG)
    qaug_ref[:, 0:HEAD_DIM] = q.astype(BF16)
    qaug_ref[:, HEAD_DIM:HEAD_DIM + LANES] = pen.astype(BF16)

    def scores(j):
        r = pl.multiple_of(j * blk, blk)
        return _nt_dot(qaug_ref[...], kaug_ref[pl.ds(r, blk), :]) * scale, r

    s, r = scores(i)
    rows = lax.broadcasted_iota(jnp.int32, s.shape, 0)
    cols = lax.broadcasted_iota(jnp.int32, s.shape, 1)
    s = jnp.where(cols <= rows, s, NEG)
    m0 = jnp.max(s, axis=1, keepdims=True)
    p = jnp.exp(s - m0)
    m_ref[...] = m0
    l_ref[...] = jnp.sum(p, axis=1, keepdims=True)
    acc_ref[...] = jnp.dot(p.astype(BF16), vb_ref[pl.ds(r, blk), :], preferred_element_type=F32)

    def body(j, carry):
        s, r = scores(j)
        m_old = m_ref[...]
        m_new = jnp.maximum(m_old, jnp.max(s, axis=1, keepdims=True))
        alpha = jnp.exp(m_old - m_new)
        p = jnp.exp(s - m_new)
        l_ref[...] = alpha * l_ref[...] + jnp.sum(p, axis=1, keepdims=True)
        acc_ref[...] = alpha * acc_ref[...] + jnp.dot(
            p.astype(BF16), vb_ref[pl.ds(r, blk), :], preferred_element_type=F32)
        m_ref[...] = m_new
        return carry

    lax.fori_loop(0, i, body, 0)
    o_ref[...] = (acc_ref[...] / l_ref[...]).astype(o_ref.dtype)


def _attn_prompt(q, k, v):
    s = q.shape[0]
    blk = MOBA_BLOCK
    assert s % blk == 0 and s // blk <= LANES
    qspec = pl.BlockSpec((blk, HEAD_DIM), lambda h, i: (i, h))
    kvspec = pl.BlockSpec((s, HEAD_DIM), lambda h, i: (0, h))
    return pl.pallas_call(
        _attn_prompt_kernel,
        grid=(N_HEADS, s // blk),
        in_specs=[qspec, kvspec, kvspec],
        out_specs=qspec,
        out_shape=jax.ShapeDtypeStruct((s, ATTN_W), BF16),
        scratch_shapes=[pltpu.VMEM((s, HEAD_DIM + LANES), BF16),
                        pltpu.VMEM((s, HEAD_DIM), BF16),
                        pltpu.VMEM((LANES, HEAD_DIM), F32),
                        pltpu.VMEM((blk, HEAD_DIM + LANES), BF16),
                        pltpu.VMEM((blk, 1), F32),
                        pltpu.VMEM((blk, 1), F32),
                        pltpu.VMEM((blk, HEAD_DIM), F32)],
        compiler_params=_params("arbitrary", "arbitrary"),
        name="attn_prompt",
    )(q, k, v)


def _diag_blocks(full, t):
    return jnp.concatenate(
        [full[h * t:(h + 1) * t, h * HEAD_DIM:(h + 1) * HEAD_DIM] for h in range(N_HEADS)], axis=0)


def _attn_sample_kernel(pt_ref, q_ref, kn_ref, vn_ref, ke_ref, ko_ref, ve_ref, vo_ref, o_ref,
                        qbd_ref, qrow_ref, g_ref, m_ref, l_ref, acc_ref):
    del pt_ref
    j = pl.program_id(1)
    nblk = pl.num_programs(1)
    t = q_ref.shape[0]
    scale = HEAD_DIM ** -0.5

    @pl.when(j == 0)
    def _():
        q = q_ref[...]
        lane_head = lax.broadcasted_iota(jnp.int32, q.shape, 1) // HEAD_DIM
        qbd = jnp.concatenate([jnp.where(lane_head == h, q, 0.0) for h in range(N_HEADS)], axis=0)
        qbd_ref[...] = qbd.astype(BF16)
        qrow_ref[...] = _diag_blocks(qbd, t)

    qbd = qbd_ref[...]

    def head_rows(row):
        return jnp.concatenate(
            [jnp.broadcast_to(row[:, h * HEAD_DIM:(h + 1) * HEAD_DIM], (t, HEAD_DIM))
             for h in range(N_HEADS)], axis=0)

    def attend(k_pages, v_pages, masks):
        ss = [_nt_dot(qbd, kp.astype(BF16)) * scale for kp in k_pages]
        ss = [s if mk is None else jnp.where(mk, s, NEG) for s, mk in zip(ss, masks)]
        m = functools.reduce(jnp.maximum, [jnp.max(s, axis=1, keepdims=True) for s in ss])
        ps = [jnp.exp(s - m) for s in ss]
        l = functools.reduce(jnp.add, [jnp.sum(p, axis=1, keepdims=True) for p in ps])
        full = functools.reduce(jnp.add, [
            jnp.dot(p.astype(BF16), vp.astype(BF16), preferred_element_type=F32)
            for p, vp in zip(ps, v_pages)])
        return m, l, _diag_blocks(full, t)

    ke = ke_ref[0]
    ko = ko_ref[0]
    kmean = (jnp.sum(ke, axis=0, keepdims=True) + jnp.sum(ko, axis=0, keepdims=True)) * (1.0 / MOBA_BLOCK)
    g_ref[j] = jnp.sum(qrow_ref[...] * head_rows(kmean), axis=1, keepdims=True)
    m, l, acc = attend([ke, ko], [ve_ref[0], vo_ref[0]], [None, None])
    m_ref[j] = m
    l_ref[j] = l
    acc_ref[j] = acc

    @pl.when(j == nblk - 1)
    def _():
        pad = jnp.zeros((PAGE_SIZE - t, kn_ref.shape[1]), F32)
        kn = jnp.concatenate([kn_ref[...], pad], axis=0)
        vn = jnp.concatenate([vn_ref[...], pad], axis=0)
        shape = (N_HEADS * t, PAGE_SIZE)
        tok = lax.broadcasted_iota(jnp.int32, shape, 0) % t
        key = lax.broadcasted_iota(jnp.int32, shape, 1)
        m_tot, l_own, acc_own = attend([kn], [vn], [key <= tok])

        n = g_ref.shape[0]
        gs = [g_ref[a] for a in range(n)]
        valid = []
        for a in range(n):
            rank = jnp.zeros(gs[a].shape, jnp.int32)
            for b in range(n):
                if b == a:
                    continue
                ahead = gs[b] > gs[a]
                if b < a:
                    ahead = jnp.logical_or(ahead, gs[b] == gs[a])
                rank = rank + ahead.astype(jnp.int32)
            valid.append(rank < MOBA_TOP_K)
        m_own = m_tot
        for a in range(n):
            m_tot = jnp.maximum(m_tot, jnp.where(valid[a], m_ref[a], NEG))
        w = jnp.exp(m_own - m_tot)
        l_tot = w * l_own
        acc = w * acc_own
        for a in range(n):
            w = jnp.where(valid[a], jnp.exp(m_ref[a] - m_tot), 0.0)
            l_tot = l_tot + w * l_ref[a]
            acc = acc + w * acc_ref[a]
        out = acc / l_tot
        for h in range(N_HEADS):
            o_ref[:, h * HEAD_DIM:(h + 1) * HEAD_DIM] = out[h * t:(h + 1) * t, :]


def _attn_sample(q, k_new, v_new, cache_k, cache_v, page_table):
    nseq, npages = page_table.shape
    t = q.shape[0] // nseq
    ppb = MOBA_BLOCK // PAGE_SIZE
    assert ppb == 2 and npages % ppb == 0 and t % SUBLANES == 0 and t <= PAGE_SIZE
    assert (npages * PAGE_SIZE) // MOBA_BLOCK == (npages * PAGE_SIZE + t - 1) // MOBA_BLOCK
    nblk = npages // ppb
    n_phys = cache_k.shape[0]
    ck = cache_k.reshape(n_phys, PAGE_SIZE, ATTN_W)
    cv = cache_v.reshape(n_phys, PAGE_SIZE, ATTN_W)
    pt = page_table.reshape(-1).astype(jnp.int32)

    new = pl.BlockSpec((t, ATTN_W), lambda b, j, pt: (b, 0))

    def page(off):
        return pl.BlockSpec((1, PAGE_SIZE, ATTN_W),
                            lambda b, j, pt: (pt[b * npages + ppb * j + off], 0, 0))

    rows = N_HEADS * t
    return pl.pallas_call(
        _attn_sample_kernel,
        grid_spec=pltpu.PrefetchScalarGridSpec(
            num_scalar_prefetch=1,
            grid=(nseq, nblk),
            in_specs=[new, new, new, page(0), page(1), page(0), page(1)],
            out_specs=new,
            scratch_shapes=[pltpu.VMEM((rows, ATTN_W), BF16),
                            pltpu.VMEM((rows, HEAD_DIM), F32),
                            pltpu.VMEM((nblk, rows, 1), F32),
                            pltpu.VMEM((nblk, rows, 1), F32),
                            pltpu.VMEM((nblk, rows, 1), F32),
                            pltpu.VMEM((nblk, rows, HEAD_DIM), F32)]),
        out_shape=jax.ShapeDtypeStruct((nseq * t, ATTN_W), F32),
        compiler_params=_params("arbitrary", "arbitrary"),
        name="attn_sample",
    )(pt, q, k_new, v_new, ck, ck, cv, cv)


CONV_HALO = 32
CONV_CHUNK = 32


def _conv_finish(y, b_ref, g_ref, beta_ref):
    y = _layer_norm(y + b_ref[...], g_ref[...], beta_ref[...])
    return y * jax.nn.sigmoid(y)


def _conv_prompt_kernel(u_ref, halo_ref, w_ref, b_ref, g_ref, beta_ref, o_ref, buf_ref):
    i = pl.program_id(0)
    tm = u_ref.shape[0]
    buf_ref[0:CONV_HALO, :] = jnp.where(i == 0, 0.0, halo_ref[...])
    buf_ref[CONV_HALO:CONV_HALO + tm, :] = u_ref[...]
    lead = CONV_HALO - (CONV_W - 1)
    for c in range(tm // CONV_CHUNK):
        base = c * CONV_CHUNK + lead
        acc = w_ref[0:1, :] * buf_ref[base:base + CONV_CHUNK, :]
        for k in range(1, CONV_W):
            acc = acc + w_ref[k:k + 1, :] * buf_ref[base + k:base + k + CONV_CHUNK, :]
        o_ref[c * CONV_CHUNK:(c + 1) * CONV_CHUNK, :] = _conv_finish(
            acc, b_ref, g_ref, beta_ref).astype(o_ref.dtype)


def _conv_prompt(u, conv_w, conv_b, ln_g, ln_b):
    s, c = u.shape
    tm = _tile(s, 256)
    assert tm % CONV_HALO == 0 and tm % CONV_CHUNK == 0
    per = tm // CONV_HALO
    full = lambda i: (0, 0)
    return pl.pallas_call(
        _conv_prompt_kernel,
        grid=(s // tm,),
        in_specs=[pl.BlockSpec((tm, c), lambda i: (i, 0)),
                  pl.BlockSpec((CONV_HALO, c), lambda i: (jnp.maximum(i * per - 1, 0), 0)),
                  pl.BlockSpec((CONV_W, c), full),
                  pl.BlockSpec((1, c), full), pl.BlockSpec((1, c), full), pl.BlockSpec((1, c), full)],
        out_specs=pl.BlockSpec((tm, c), lambda i: (i, 0)),
        out_shape=jax.ShapeDtypeStruct((s, c), BF16),
        scratch_shapes=[pltpu.VMEM((CONV_HALO + tm, c), F32)],
        compiler_params=_params("arbitrary"),
        name="conv_prompt",
    )(u, u, conv_w, conv_b, ln_g, ln_b)


def _conv_sample_kernel(hist_ref, w_ref, b_ref, g_ref, beta_ref, o_ref):
    t = o_ref.shape[1]
    for s in range(hist_ref.shape[0]):
        acc = w_ref[0:1, :] * hist_ref[s, 0:t, :]
        for k in range(1, CONV_W):
            acc = acc + w_ref[k:k + 1, :] * hist_ref[s, k:k + t, :]
        o_ref[s] = _conv_finish(acc, b_ref, g_ref, beta_ref)


def _conv_sample(hist, conv_w, conv_b, ln_g, ln_b):
    nseq, rows, c = hist.shape
    t = rows - (CONV_W - 1)
    gs = _tile(nseq, 8)
    full = lambda i: (0, 0)
    return pl.pallas_call(
        _conv_sample_kernel,
        grid=(nseq // gs,),
        in_specs=[pl.BlockSpec((gs, rows, c), lambda i: (i, 0, 0)),
                  pl.BlockSpec((CONV_W, c), full),
                  pl.BlockSpec((1, c), full), pl.BlockSpec((1, c), full), pl.BlockSpec((1, c), full)],
        out_specs=pl.BlockSpec((gs, t, c), lambda i: (i, 0, 0)),
        out_shape=jax.ShapeDtypeStruct((nseq, t, c), F32),
        compiler_params=_params("arbitrary"),
        name="conv_sample",
    )(hist, conv_w, conv_b, ln_g, ln_b)


def _outproj_kernel(alpha, attn_ref, conv_ref, x_ref, wa_ref, wc_ref, g_ref, b_ref, h_ref, hb_ref):
    mixed = jnp.dot(attn_ref[...].astype(BF16), wa_ref[...], preferred_element_type=F32)
    mixed = mixed + jnp.dot(conv_ref[...].astype(BF16), wc_ref[...], preferred_element_type=F32)
    h = _layer_norm(alpha * x_ref[...] + mixed, g_ref[...], b_ref[...])
    h_ref[...] = h
    hb_ref[...] = h.astype(BF16)


def _outproj(alpha, attn, conv, x, w_out_b, ln_g, ln_b):
    n, d = x.shape
    wa = attn.shape[1]
    wc = conv.shape[1]
    tm = _tile(n, 256)
    row = lambda i: (i, 0)
    full = lambda i: (0, 0)
    return pl.pallas_call(
        functools.partial(_outproj_kernel, alpha),
        grid=(n // tm,),
        in_specs=[pl.BlockSpec((tm, wa), row), pl.BlockSpec((tm, wc), row), pl.BlockSpec((tm, d), row),
                  pl.BlockSpec((wa, d), full), pl.BlockSpec((wc, d), lambda i: (wa // wc, 0)),
                  pl.BlockSpec((1, d), full), pl.BlockSpec((1, d), full)],
        out_specs=[pl.BlockSpec((tm, d), row)] * 2,
        out_shape=[jax.ShapeDtypeStruct((n, d), F32), jax.ShapeDtypeStruct((n, d), BF16)],
        compiler_params=_params("arbitrary"),
        name="outproj",
    )(attn, conv, x, w_out_b, w_out_b, ln_g, ln_b)


def _top_sorted(x, k):
    out = []
    for _ in range(k):
        mx = jnp.max(x, axis=0, keepdims=True)
        out.append(mx)
        x = jnp.where(x >= mx, -jnp.inf, x)
    return out


def _route_kernel(hb_ref, wq_ref, keys_ref, c_ref, bs_ref, wa_ref, eb_ref):
    q = jnp.dot(hb_ref[...], wq_ref[...], preferred_element_type=F32)
    k = PEER_TOPK
    for h in range(PEER_HEADS):
        lo = h * 2 * PEER_HALF
        sa = _nt_dot(keys_ref[h, 0], q[:, lo:lo + PEER_HALF], precision=lax.Precision.HIGHEST)
        sb = _nt_dot(keys_ref[h, 1], q[:, lo + PEER_HALF:lo + 2 * PEER_HALF],
                     precision=lax.Precision.HIGHEST)
        ta = _top_sorted(sa, k + 1)
        tb_rows = _top_sorted(sb, k + 1)
        tb = jnp.concatenate(tb_rows[:k], axis=0)
        half = k // 2
        cand = [ta[0] + tb]
        cand += [ta[i] + tb[0:half] for i in range(1, half)]
        cand += [jnp.concatenate(ta[half:k], axis=0) + tb[0:1]]
        row = lax.broadcasted_iota(jnp.int32, (SUBLANES, sa.shape[1]), 0)
        cand += [jnp.where(row == 0, ta[k] + tb_rows[0],
                           jnp.where(row == 1, ta[0] + tb_rows[k], -jnp.inf))]
        cand = jnp.concatenate(cand, axis=0)
        best = _top_sorted(cand, k + 1)
        tau = 0.5 * (best[k - 1] + best[k])
        top = ta[0] + tb[0:1]
        z = jnp.sum(jnp.where(cand >= tau, jnp.exp(cand - top), 0.0), axis=0, keepdims=True)
        c_ref[h] = tau - sa
        bs_ref[h] = sb
        wa_ref[h] = jnp.exp(sa - ta[0]) / z
        eb_ref[h] = jnp.exp(sb - tb[0:1])


def _route(hb, wq_b, keys):
    n, d = hb.shape
    tm = _tile(n, 256)
    out = jax.ShapeDtypeStruct((PEER_HEADS, PEER_KEYS, n), F32)
    ospec = pl.BlockSpec((PEER_HEADS, PEER_KEYS, tm), lambda i: (0, 0, i))
    return pl.pallas_call(
        _route_kernel,
        grid=(n // tm,),
        in_specs=[pl.BlockSpec((tm, d), lambda i: (i, 0)),
                  pl.BlockSpec(wq_b.shape, lambda i: (0, 0)),
                  pl.BlockSpec(keys.shape, lambda i: (0, 0, 0, 0))],
        out_specs=[ospec] * 4,
        out_shape=[out] * 4,
        compiler_params=_params("arbitrary"),
        name="route",
    )(hb, wq_b, keys)


def _expert_kernel(hb_ref, u_ref, vt_ref, c_ref, bs_ref, wa_ref, eb_ref, o_ref, acc_ref):
    e = pl.program_id(1)
    groups = u_ref.shape[0] // PEER_KEYS

    @pl.when(e == 0)
    def _():
        acc_ref[...] = jnp.zeros_like(acc_ref)

    xu = _nt_dot(u_ref[...], hb_ref[...])
    act = 0.5 * xu * (1.0 + lax.erf(xu * (1.0 / math.sqrt(2.0))))
    parts = []
    for r in range(groups):
        a = e * groups + r
        g = None
        for h in range(PEER_HEADS):
            c = c_ref[h, pl.ds(a, 1), :]
            w = wa_ref[h, pl.ds(a, 1), :]
            term = jnp.where(bs_ref[h] >= c, eb_ref[h], 0.0) * w
            g = term if g is None else g + term
        parts.append(g)
    wgt = (jnp.concatenate(parts, axis=0) * act).astype(BF16)
    acc_ref[...] += jnp.dot(vt_ref[...], wgt, preferred_element_type=F32)

    @pl.when(e == pl.num_programs(1) - 1)
    def _():
        o_ref[...] = acc_ref[...].T


def _expert(hb, u_b, vt_b, c, bs, wa, eb):
    n, d = hb.shape
    ne = u_b.shape[0]
    tm = _tile(n, 512)
    te = _tile(ne, 512)
    rspec = pl.BlockSpec((PEER_HEADS, PEER_KEYS, tm), lambda i, e: (0, 0, i))
    return pl.pallas_call(
        _expert_kernel,
        grid=(n // tm, ne // te),
        in_specs=[pl.BlockSpec((tm, d), lambda i, e: (i, 0)),
                  pl.BlockSpec((te, d), lambda i, e: (e, 0)),
                  pl.BlockSpec((d, te), lambda i, e: (0, e)),
                  rspec, rspec, rspec, rspec],
        out_specs=pl.BlockSpec((tm, d), lambda i, e: (i, 0)),
        out_shape=jax.ShapeDtypeStruct((n, d), F32),
        scratch_shapes=[pltpu.VMEM((d, tm), F32)],
        compiler_params=_params("arbitrary", "arbitrary"),
        name="expert",
    )(hb, u_b, vt_b, c, bs, wa, eb)


def _final_kernel(alpha, h_ref, peer_ref, p_ref, gw_ref, gb_ref, pw_ref, g_ref, b_ref, y_ref):
    h2 = _layer_norm(alpha * h_ref[...] + peer_ref[...], g_ref[...], b_ref[...])
    gate = jax.nn.sigmoid(jnp.dot(h2.astype(BF16), gw_ref[...], preferred_element_type=F32) + gb_ref[...])
    emb = jnp.dot(p_ref[...].astype(BF16), pw_ref[...], preferred_element_type=F32)
    y_ref[...] = h2 + gate * emb


def _final(alpha, h, peer_out, p, gate_w_b, gate_b, ple_w_b, ln_g, ln_b):
    n, d = h.shape
    pd = p.shape[1]
    tm = _tile(n, 256)
    row = lambda i: (i, 0)
    full = lambda i: (0, 0)
    return pl.pallas_call(
        functools.partial(_final_kernel, alpha),
        grid=(n // tm,),
        in_specs=[pl.BlockSpec((tm, d), row), pl.BlockSpec((tm, d), row), pl.BlockSpec((tm, pd), row),
                  pl.BlockSpec((d, d), full), pl.BlockSpec((1, d), full), pl.BlockSpec((pd, d), full),
                  pl.BlockSpec((1, d), full), pl.BlockSpec((1, d), full)],
        out_specs=pl.BlockSpec((tm, d), row),
        out_shape=jax.ShapeDtypeStruct((n, d), F32),
        compiler_params=_params("arbitrary"),
        name="final",
    )(h, peer_out, p, gate_w_b, gate_b, ple_w_b, ln_g, ln_b)


def _channel_mix(alpha, x, p, attn, conv, lw):
    h, hb = _outproj(alpha, attn, conv, x, lw["w_out"], lw["ln1_g"], lw["ln1_b"])
    c, bs, wa, eb = _route(hb, lw["peer_wq"], lw["peer_keys"])
    peer_out = _expert(hb, lw["peer_u"], lw["peer_vt"], c, bs, wa, eb)
    return _final(alpha, h, peer_out, p, lw["gate_w"], lw["gate_b"], lw["ple_w"], lw["ln2_g"], lw["ln2_b"])


def kernel(x_prompt, x_sample, cache_k, cache_v, state_conv, page_table, p_prompt, p_sample, w_in, w_out, conv_w, conv_b, conv_ln_g, conv_ln_b, ln1_g, ln1_b, peer_wq, peer_keys, peer_u, peer_v, ln2_g, ln2_b, ple_w, gate_w, gate_b):
    depth = w_in.shape[0]
    batch, seq, d = x_prompt.shape
    nseq, dec_seq, _ = x_sample.shape
    c_conv = conv_w.shape[2]
    past_len = page_table.shape[1] * PAGE_SIZE
    alpha = (2.0 * depth) ** 0.25
    assert batch == 1 and w_in.shape[2] == 3 * ATTN_W + 2 * c_conv and c_conv == ATTN_W
    assert peer_keys.shape[1:] == (PEER_HEADS, 2, PEER_KEYS, PEER_HALF)

    cos_p, sin_p = _rotary_tables(jnp.arange(seq, dtype=jnp.int32))
    cos_s, sin_s = _rotary_tables(past_len + jnp.arange(dec_seq, dtype=jnp.int32))
    cos_s = jnp.tile(cos_s, (nseq, 1))
    sin_s = jnp.tile(sin_s, (nseq, 1))

    xp = x_prompt.reshape(seq, d)
    xs = x_sample.reshape(nseq * dec_seq, d)
    outs = [[] for _ in range(6)]
    for i in range(depth):
        vec = lambda a: a[i].reshape(1, -1)
        lw = dict(w_out=w_out[i].astype(BF16), ln1_g=vec(ln1_g), ln1_b=vec(ln1_b),
                  peer_wq=peer_wq[i].astype(BF16), peer_keys=peer_keys[i],
                  peer_u=peer_u[i].astype(BF16), peer_vt=peer_v[i].astype(BF16).T,
                  gate_w=gate_w[i].astype(BF16), gate_b=vec(gate_b), ple_w=ple_w[i].astype(BF16),
                  ln2_g=vec(ln2_g), ln2_b=vec(ln2_b))
        cw = (conv_w[i], vec(conv_b), vec(conv_ln_g), vec(conv_ln_b))
        w_in_b = w_in[i].astype(BF16)

        qp, kp, vp, up = _proj(xp, w_in_b, cos_p, sin_p)
        attn_p = _attn_prompt(qp, kp, vp)
        conv_p = _conv_prompt(up, *cw)
        xp_next = _channel_mix(alpha, xp, p_prompt[i].reshape(seq, -1), attn_p, conv_p, lw)

        qs, ks, vs, us = _proj(xs, w_in_b, cos_s, sin_s)
        attn_s = _attn_sample(qs, ks, vs, cache_k[i], cache_v[i], page_table)
        hist_s = jnp.concatenate([state_conv[i], us.reshape(nseq, dec_seq, c_conv)], axis=1)
        conv_s = _conv_sample(hist_s, *cw).reshape(nseq * dec_seq, c_conv)
        xs = _channel_mix(alpha, xs, p_sample[i].reshape(nseq * dec_seq, -1), attn_s, conv_s, lw)
        xp = xp_next

        hist_tail = jnp.concatenate([jnp.zeros((CONV_W - 1, c_conv), F32), up], axis=0)[-(CONV_W - 1):]
        outs[0].append(kp.reshape(batch, seq, N_HEADS, HEAD_DIM))
        outs[1].append(vp.reshape(batch, seq, N_HEADS, HEAD_DIM))
        outs[2].append(hist_tail.reshape(batch, CONV_W - 1, c_conv))
        outs[3].append(ks.reshape(nseq, dec_seq, N_HEADS, HEAD_DIM))
        outs[4].append(vs.reshape(nseq, dec_seq, N_HEADS, HEAD_DIM))
        outs[5].append(hist_s[:, -(CONV_W - 1):])
    kp_l, vp_l, cp_l, ks_l, vs_l, cs_l = [jnp.stack(o) for o in outs]
    return (xp.reshape(batch, seq, d), xs.reshape(nseq, dec_seq, d), kp_l, vp_l, cp_l, ks_l, vs_l, cs_l)
```

```python
import functools
import math

import jax
import jax.numpy as jnp
from jax import lax
from jax.experimental import pallas as pl
from jax.experimental.pallas import tpu as pltpu

N_HEADS = 8
HEAD_DIM = 128
ATTN_W = N_HEADS * HEAD_DIM
CONV_W = 31
MOBA_BLOCK = 256
MOBA_TOP_K = 3
ROPE_THETA = 10000.0
PAGE_SIZE = 128
PEER_HEADS = 8
PEER_KEYS = 128
PEER_TOPK = 16
PEER_HALF = 128
LN_EPS = 1e-5

LANES = 128
SUBLANES = 8
MXU_DEPTH = 256
VMEM_LIMIT = 56 * 1024 * 1024

NEG = -1e30
F32 = jnp.float32
BF16 = jnp.bfloat16
NT_DIMS = (((1,), (1,)), ((), ()))


def _nt_dot(a, b, precision=None):
    return lax.dot_general(a, b, NT_DIMS, precision=precision, preferred_element_type=F32)


def _layer_norm(x, g, b):
    mu = jnp.mean(x, axis=-1, keepdims=True)
    xc = x - mu
    var = jnp.mean(xc * xc, axis=-1, keepdims=True)
    return xc * lax.rsqrt(var + LN_EPS) * g + b


def _params(*sem):
    return pltpu.CompilerParams(dimension_semantics=sem, vmem_limit_bytes=VMEM_LIMIT)


def _tile(n, t):
    t = min(n, t)
    assert n % t == 0, (n, t)
    return t


def _proj_kernel(x_ref, w_ref, cos_ref, sin_ref, q_ref, k_ref, v_ref, u_ref, xb_ref, a_ref):
    j = pl.program_id(1)

    @pl.when(j == 0)
    def _():
        xb_ref[...] = x_ref[...].astype(BF16)

    z = jnp.dot(xb_ref[...], w_ref[...], preferred_element_type=F32)

    def rotary_to(o_ref):
        c = cos_ref[...]
        s = sin_ref[...]
        for h in range(N_HEADS):
            zh = z[:, h * HEAD_DIM:(h + 1) * HEAD_DIM]
            o_ref[:, h * HEAD_DIM:(h + 1) * HEAD_DIM] = zh * c + pltpu.roll(zh, HEAD_DIM // 2, 1) * s

    @pl.when(j == 0)
    def _():
        rotary_to(q_ref)

    @pl.when(j == 1)
    def _():
        rotary_to(k_ref)

    @pl.when(j == 2)
    def _():
        v_ref[...] = z

    @pl.when(j == 3)
    def _():
        a_ref[...] = z

    @pl.when(j == 4)
    def _():
        u_ref[...] = a_ref[...] * jax.nn.sigmoid(z)


def _proj(x, w_in_b, cos_t, sin_t):
    n, d = x.shape
    tm = _tile(n, 512)
    row = lambda i, j: (i, 0)
    out = jax.ShapeDtypeStruct((n, ATTN_W), F32)
    return pl.pallas_call(
        _proj_kernel,
        grid=(n // tm, 5),
        in_specs=[pl.BlockSpec((tm, d), row),
                  pl.BlockSpec((d, ATTN_W), lambda i, j: (0, j)),
                  pl.BlockSpec((tm, HEAD_DIM), row),
                  pl.BlockSpec((tm, HEAD_DIM), row)],
        out_specs=[pl.BlockSpec((tm, ATTN_W), row)] * 4,
        out_shape=[out] * 4,
        scratch_shapes=[pltpu.VMEM((tm, d), BF16), pltpu.VMEM((tm, ATTN_W), F32)],
        compiler_params=_params("arbitrary", "arbitrary"),
        name="proj",
    )(x, w_in_b, cos_t, sin_t)


def _rotary_tables(pos):
    half = HEAD_DIM // 2
    inv = ROPE_THETA ** (-jnp.arange(half, dtype=F32) * 2.0 / HEAD_DIM)
    ang = pos.astype(F32)[:, None] * inv[None, :]
    c, s = jnp.cos(ang), jnp.sin(ang)
    return jnp.concatenate([c, c], axis=1), jnp.concatenate([-s, s], axis=1)


EXP2_SCALE = HEAD_DIM ** -0.5 * math.log2(math.e)


def _attn_prompt_kernel(grp, q_ref, k_ref, v_ref, o_ref,
                        kaug_ref, vt_ref, kmean_ref, qaug_ref, m_ref, l_ref, acc_ref):
    i = pl.program_id(1)
    blk = MOBA_BLOCK
    nb = k_ref.shape[0] // blk
    nbp = -(-nb // SUBLANES) * SUBLANES

    @pl.when(i == 0)
    def _():
        kmean_ref[...] = jnp.zeros_like(kmean_ref)

        def fill(j, carry):
            r = pl.multiple_of(j * blk, blk)
            kb = k_ref[pl.ds(r, blk), :]
            kmean_ref[pl.ds(j, 1), :] = jnp.sum(kb, axis=0, keepdims=True) * (1.0 / blk)
            hot = lax.broadcasted_iota(jnp.int32, (blk, LANES), 1) == j
            kaug_ref[pl.ds(r, blk), 0:HEAD_DIM] = kb.astype(BF16)
            kaug_ref[pl.ds(r, blk), HEAD_DIM:HEAD_DIM + LANES] = hot.astype(F32).astype(BF16)
            vt_ref[:, pl.ds(r, blk)] = v_ref[pl.ds(r, blk), :].T.astype(BF16)
            return carry

        lax.fori_loop(0, nb, fill, 0)

    q = q_ref[...]
    gate = _nt_dot(kmean_ref[0:nbp, :], q, precision=lax.Precision.HIGHEST)
    row = lax.broadcasted_iota(jnp.int32, gate.shape, 0)
    past = row < i
    work = jnp.where(past, gate, -jnp.inf)
    sel = jnp.zeros(gate.shape, jnp.bool_)
    for _ in range(MOBA_TOP_K):
        mx = jnp.max(work, axis=0, keepdims=True)
        idx = jnp.min(jnp.where(work == mx, row, jnp.int32(1 << 30)), axis=0, keepdims=True)
        pick = row == idx
        sel = jnp.logical_or(sel, pick)
        work = jnp.where(pick, -jnp.inf, work)
    keep = jnp.logical_or(jnp.logical_and(sel, past), row == i)
    pen = jnp.where(keep, 0.0, NEG)
    pen = jnp.concatenate([pen, jnp.zeros((LANES - nbp, blk), F32)], axis=0)
    qaug_ref[0:HEAD_DIM, :] = q.T.astype(BF16)
    qaug_ref[HEAD_DIM:HEAD_DIM + LANES, :] = pen.astype(BF16)

    gk = grp * blk
    gi = i // grp

    def scores(g):
        r = pl.multiple_of(g * gk, gk)
        s = jnp.dot(kaug_ref[pl.ds(r, gk), :], qaug_ref[...], preferred_element_type=F32)
        return s, r

    s, r = scores(gi)
    key_pos = lax.broadcasted_iota(jnp.int32, s.shape, 0) + gi * gk
    qry_pos = lax.broadcasted_iota(jnp.int32, s.shape, 1) + i * blk
    s = jnp.where(key_pos <= qry_pos, s, NEG)
    m0 = jnp.max(s, axis=0, keepdims=True)
    p = jnp.exp2((s - m0) * EXP2_SCALE)
    m_ref[...] = m0
    l_ref[...] = jnp.sum(p, axis=0, keepdims=True)
    acc_ref[...] = jnp.dot(vt_ref[:, pl.ds(r, gk)], p.astype(BF16), preferred_element_type=F32)

    def body(g, carry):
        s, r = scores(g)
        m_old = m_ref[...]
        m_new = jnp.maximum(m_old, jnp.max(s, axis=0, keepdims=True))
        alpha = jnp.exp2((m_old - m_new) * EXP2_SCALE)
        p = jnp.exp2((s - m_new) * EXP2_SCALE)
        l_ref[...] = alpha * l_ref[...] + jnp.sum(p, axis=0, keepdims=True)
        acc_ref[...] = alpha * acc_ref[...] + jnp.dot(
            vt_ref[:, pl.ds(r, gk)], p.astype(BF16), preferred_element_type=F32)
        m_ref[...] = m_new
        return carry

    lax.fori_loop(0, gi, body, 0)
    o_ref[...] = (acc_ref[...] / l_ref[...]).T.astype(o_ref.dtype)


def _attn_prompt(q, k, v):
    s = q.shape[0]
    blk = MOBA_BLOCK
    nb = s // blk
    assert s % blk == 0 and nb <= LANES
    grp = next(g for g in (4, 2, 1) if nb % g == 0)
    qspec = pl.BlockSpec((blk, HEAD_DIM), lambda h, i: (i, h))
    kvspec = pl.BlockSpec((s, HEAD_DIM), lambda h, i: (0, h))
    return pl.pallas_call(
        functools.partial(_attn_prompt_kernel, grp),
        grid=(N_HEADS, nb),
        in_specs=[qspec, kvspec, kvspec],
        out_specs=qspec,
        out_shape=jax.ShapeDtypeStruct((s, ATTN_W), BF16),
        scratch_shapes=[pltpu.VMEM((s, HEAD_DIM + LANES), BF16),
                        pltpu.VMEM((HEAD_DIM, s), BF16),
                        pltpu.VMEM((LANES, HEAD_DIM), F32),
                        pltpu.VMEM((HEAD_DIM + LANES, blk), BF16),
                        pltpu.VMEM((1, blk), F32),
                        pltpu.VMEM((1, blk), F32),
                        pltpu.VMEM((HEAD_DIM, blk), F32)],
        compiler_params=_params("arbitrary", "arbitrary"),
        name="attn_prompt",
    )(q, k, v)


def _attn_sample_kernel(pt_ref, q_ref, kn_ref, vn_ref, ke_ref, ko_ref, ve_ref, vo_ref, o_ref,
                        g_ref, m_ref, l_ref, acc_ref):
    del pt_ref
    j = pl.program_id(1)
    nblk = pl.num_programs(1)
    t = q_ref.shape[0]
    scale = HEAD_DIM ** -0.5
    heads = range(N_HEADS)
    q = q_ref[...]
    qh = [q[:, h * HEAD_DIM:(h + 1) * HEAD_DIM] for h in heads]

    def attend(k_heads, v_heads, mask):
        s = jnp.concatenate([_nt_dot(qh[h].astype(BF16), k_heads[h].astype(BF16)) for h in heads], axis=0)
        s = s * scale
        if mask is not None:
            s = jnp.where(mask, s, NEG)
        m = jnp.max(s, axis=1, keepdims=True)
        p = jnp.exp(s - m)
        l = jnp.sum(p, axis=1, keepdims=True)
        acc = jnp.concatenate([
            jnp.dot(p[h * t:(h + 1) * t].astype(BF16), v_heads[h].astype(BF16), preferred_element_type=F32)
            for h in heads], axis=0)
        return m, l, acc

    def paged(even_ref, odd_ref):
        return [jnp.concatenate([r[0, 0, pl.ds(h, PAGE_SIZE, stride=N_HEADS), :] for r in (even_ref, odd_ref)],
                                axis=0) for h in heads]

    k_heads = paged(ke_ref, ko_ref)
    g_ref[j] = jnp.concatenate([
        jnp.sum(qh[h] * (jnp.sum(k_heads[h], axis=0, keepdims=True) * (1.0 / MOBA_BLOCK)), axis=1, keepdims=True)
        for h in heads], axis=0)
    m, l, acc = attend(k_heads, paged(ve_ref, vo_ref), None)
    m_ref[j] = m
    l_ref[j] = l
    acc_ref[j] = acc

    @pl.when(j == nblk - 1)
    def _():
        pad = jnp.zeros((PAGE_SIZE - t, HEAD_DIM), F32)
        kn = [jnp.concatenate([kn_ref[:, h * HEAD_DIM:(h + 1) * HEAD_DIM], pad], axis=0) for h in heads]
        vn = [jnp.concatenate([vn_ref[:, h * HEAD_DIM:(h + 1) * HEAD_DIM], pad], axis=0) for h in heads]
        shape = (N_HEADS * t, PAGE_SIZE)
        tok = lax.broadcasted_iota(jnp.int32, shape, 0) % t
        key = lax.broadcasted_iota(jnp.int32, shape, 1)
        m_own, l_own, acc_own = attend(kn, vn, key <= tok)

        n = g_ref.shape[0]
        gs = [g_ref[a] for a in range(n)]
        valid = []
        for a in range(n):
            rank = jnp.zeros(gs[a].shape, jnp.int32)
            for b in range(n):
                if b == a:
                    continue
                ahead = gs[b] > gs[a]
                if b < a:
                    ahead = jnp.logical_or(ahead, gs[b] == gs[a])
                rank = rank + ahead.astype(jnp.int32)
            valid.append(rank < MOBA_TOP_K)
        m_tot = m_own
        for a in range(n):
            m_tot = jnp.maximum(m_tot, jnp.where(valid[a], m_ref[a], NEG))
        w = jnp.exp(m_own - m_tot)
        l_tot = w * l_own
        acc = w * acc_own
        for a in range(n):
            w = jnp.where(valid[a], jnp.exp(m_ref[a] - m_tot), 0.0)
            l_tot = l_tot + w * l_ref[a]
            acc = acc + w * acc_ref[a]
        out = acc / l_tot
        for h in heads:
            o_ref[:, h * HEAD_DIM:(h + 1) * HEAD_DIM] = out[h * t:(h + 1) * t, :]


def _attn_sample(layer, q, k_new, v_new, cache_k, cache_v, page_table):
    nseq, npages = page_table.shape
    t = q.shape[0] // nseq
    ppb = MOBA_BLOCK // PAGE_SIZE
    assert ppb == 2 and npages % ppb == 0 and t % SUBLANES == 0 and t <= PAGE_SIZE
    assert (npages * PAGE_SIZE) // MOBA_BLOCK == (npages * PAGE_SIZE + t - 1) // MOBA_BLOCK
    assert cache_k.shape[2:] == (PAGE_SIZE, N_HEADS, HEAD_DIM)
    nblk = npages // ppb
    pt = page_table.reshape(-1).astype(jnp.int32)
    page_rows = PAGE_SIZE * N_HEADS
    cache_k = cache_k.reshape(cache_k.shape[0], cache_k.shape[1], page_rows, HEAD_DIM)
    cache_v = cache_v.reshape(cache_v.shape[0], cache_v.shape[1], page_rows, HEAD_DIM)

    new = pl.BlockSpec((t, ATTN_W), lambda b, j, pt: (b, 0))

    def page(off):
        return pl.BlockSpec((1, 1, page_rows, HEAD_DIM),
                            lambda b, j, pt: (layer, pt[b * npages + ppb * j + off], 0, 0))

    rows = N_HEADS * t
    return pl.pallas_call(
        _attn_sample_kernel,
        grid_spec=pltpu.PrefetchScalarGridSpec(
            num_scalar_prefetch=1,
            grid=(nseq, nblk),
            in_specs=[new, new, new, page(0), page(1), page(0), page(1)],
            out_specs=new,
            scratch_shapes=[pltpu.VMEM((nblk, rows, 1), F32),
                            pltpu.VMEM((nblk, rows, 1), F32),
                            pltpu.VMEM((nblk, rows, 1), F32),
                            pltpu.VMEM((nblk, rows, HEAD_DIM), F32)]),
        out_shape=jax.ShapeDtypeStruct((nseq * t, ATTN_W), F32),
        compiler_params=_params("arbitrary", "arbitrary"),
        name="attn_sample",
    )(pt, q, k_new, v_new, cache_k, cache_k, cache_v, cache_v)


CONV_HALO = 32
CONV_CHUNK = 32


def _conv_finish(y, b_ref, g_ref, beta_ref):
    y = _layer_norm(y + b_ref[...], g_ref[...], beta_ref[...])
    return y * jax.nn.sigmoid(y)


def _conv_prompt_kernel(u_ref, halo_ref, w_ref, b_ref, g_ref, beta_ref, o_ref, buf_ref):
    i = pl.program_id(0)
    tm = u_ref.shape[0]
    buf_ref[0:CONV_HALO, :] = jnp.where(i == 0, 0.0, halo_ref[...])
    buf_ref[CONV_HALO:CONV_HALO + tm, :] = u_ref[...]
    lead = CONV_HALO - (CONV_W - 1)
    for c in range(tm // CONV_CHUNK):
        base = c * CONV_CHUNK + lead
        acc = w_ref[0:1, :] * buf_ref[base:base + CONV_CHUNK, :]
        for k in range(1, CONV_W):
            acc = acc + w_ref[k:k + 1, :] * buf_ref[base + k:base + k + CONV_CHUNK, :]
        o_ref[c * CONV_CHUNK:(c + 1) * CONV_CHUNK, :] = _conv_finish(
            acc, b_ref, g_ref, beta_ref).astype(o_ref.dtype)


def _conv_prompt(u, conv_w, conv_b, ln_g, ln_b):
    s, c = u.shape
    tm = _tile(s, 256)
    assert tm % CONV_HALO == 0 and tm % CONV_CHUNK == 0
    per = tm // CONV_HALO
    full = lambda i: (0, 0)
    return pl.pallas_call(
        _conv_prompt_kernel,
        grid=(s // tm,),
        in_specs=[pl.BlockSpec((tm, c), lambda i: (i, 0)),
                  pl.BlockSpec((CONV_HALO, c), lambda i: (jnp.maximum(i * per - 1, 0), 0)),
                  pl.BlockSpec((CONV_W, c), full),
                  pl.BlockSpec((1, c), full), pl.BlockSpec((1, c), full), pl.BlockSpec((1, c), full)],
        out_specs=pl.BlockSpec((tm, c), lambda i: (i, 0)),
        out_shape=jax.ShapeDtypeStruct((s, c), BF16),
        scratch_shapes=[pltpu.VMEM((CONV_HALO + tm, c), F32)],
        compiler_params=_params("arbitrary"),
        name="conv_prompt",
    )(u, u, conv_w, conv_b, ln_g, ln_b)


def _conv_sample_kernel(hist_ref, w_ref, b_ref, g_ref, beta_ref, o_ref):
    t = o_ref.shape[1]
    for s in range(hist_ref.shape[0]):
        acc = w_ref[0:1, :] * hist_ref[s, 0:t, :]
        for k in range(1, CONV_W):
            acc = acc + w_ref[k:k + 1, :] * hist_ref[s, k:k + t, :]
        o_ref[s] = _conv_finish(acc, b_ref, g_ref, beta_ref)


def _conv_sample(hist, conv_w, conv_b, ln_g, ln_b):
    nseq, rows, c = hist.shape
    t = rows - (CONV_W - 1)
    gs = _tile(nseq, 8)
    full = lambda i: (0, 0)
    return pl.pallas_call(
        _conv_sample_kernel,
        grid=(nseq // gs,),
        in_specs=[pl.BlockSpec((gs, rows, c), lambda i: (i, 0, 0)),
                  pl.BlockSpec((CONV_W, c), full),
                  pl.BlockSpec((1, c), full), pl.BlockSpec((1, c), full), pl.BlockSpec((1, c), full)],
        out_specs=pl.BlockSpec((gs, t, c), lambda i: (i, 0, 0)),
        out_shape=jax.ShapeDtypeStruct((nseq, t, c), F32),
        compiler_params=_params("arbitrary"),
        name="conv_sample",
    )(hist, conv_w, conv_b, ln_g, ln_b)


def _outproj_kernel(alpha, attn_ref, conv_ref, x_ref, wa_ref, wc_ref, g_ref, b_ref, h_ref, hb_ref):
    mixed = jnp.dot(attn_ref[...].astype(BF16), wa_ref[...], preferred_element_type=F32)
    mixed = mixed + jnp.dot(conv_ref[...].astype(BF16), wc_ref[...], preferred_element_type=F32)
    h = _layer_norm(alpha * x_ref[...] + mixed, g_ref[...], b_ref[...])
    h_ref[...] = h
    hb_ref[...] = h.astype(BF16)


def _outproj(alpha, attn, conv, x, w_out_b, ln_g, ln_b):
    n, d = x.shape
    wa = attn.shape[1]
    wc = conv.shape[1]
    tm = _tile(n, 256)
    row = lambda i: (i, 0)
    full = lambda i: (0, 0)
    return pl.pallas_call(
        functools.partial(_outproj_kernel, alpha),
        grid=(n // tm,),
        in_specs=[pl.BlockSpec((tm, wa), row), pl.BlockSpec((tm, wc), row), pl.BlockSpec((tm, d), row),
                  pl.BlockSpec((wa, d), full), pl.BlockSpec((wc, d), lambda i: (wa // wc, 0)),
                  pl.BlockSpec((1, d), full), pl.BlockSpec((1, d), full)],
        out_specs=[pl.BlockSpec((tm, d), row)] * 2,
        out_shape=[jax.ShapeDtypeStruct((n, d), F32), jax.ShapeDtypeStruct((n, d), BF16)],
        compiler_params=_params("arbitrary"),
        name="outproj",
    )(attn, conv, x, w_out_b, w_out_b, ln_g, ln_b)


def _top_sorted(x, k):
    out = []
    for _ in range(k):
        mx = jnp.max(x, axis=0, keepdims=True)
        out.append(mx)
        x = jnp.where(x >= mx, -jnp.inf, x)
    return out


def _route_kernel(hb_ref, wq_ref, keys_ref, c_ref, bs_ref, wa_ref, eb_ref):
    q = jnp.dot(hb_ref[...], wq_ref[...], preferred_element_type=F32)
    k = PEER_TOPK
    for h in range(PEER_HEADS):
        lo = h * 2 * PEER_HALF
        sa = _nt_dot(keys_ref[h, 0], q[:, lo:lo + PEER_HALF], precision=lax.Precision.HIGHEST)
        sb = _nt_dot(keys_ref[h, 1], q[:, lo + PEER_HALF:lo + 2 * PEER_HALF],
                     precision=lax.Precision.HIGHEST)
        ta = _top_sorted(sa, k + 1)
        tb_rows = _top_sorted(sb, k + 1)
        tb = jnp.concatenate(tb_rows[:k], axis=0)
        half = k // 2
        cand = [ta[0] + tb]
        cand += [ta[i] + tb[0:half] for i in range(1, half)]
        cand += [jnp.concatenate(ta[half:k], axis=0) + tb[0:1]]
        row = lax.broadcasted_iota(jnp.int32, (SUBLANES, sa.shape[1]), 0)
        cand += [jnp.where(row == 0, ta[k] + tb_rows[0],
                           jnp.where(row == 1, ta[0] + tb_rows[k], -jnp.inf))]
        cand = jnp.concatenate(cand, axis=0)
        best = _top_sorted(cand, k + 1)
        tau = 0.5 * (best[k - 1] + best[k])
        top = ta[0] + tb[0:1]
        z = jnp.sum(jnp.where(cand >= tau, jnp.exp(cand - top), 0.0), axis=0, keepdims=True)
        c_ref[h] = tau - sa
        bs_ref[h] = sb
        wa_ref[h] = jnp.exp(sa - ta[0]) / z
        eb_ref[h] = jnp.exp(sb - tb[0:1])


def _route(hb, wq_b, keys):
    n, d = hb.shape
    tm = _tile(n, 256)
    out = jax.ShapeDtypeStruct((PEER_HEADS, PEER_KEYS, n), F32)
    ospec = pl.BlockSpec((PEER_HEADS, PEER_KEYS, tm), lambda i: (0, 0, i))
    return pl.pallas_call(
        _route_kernel,
        grid=(n // tm,),
        in_specs=[pl.BlockSpec((tm, d), lambda i: (i, 0)),
                  pl.BlockSpec(wq_b.shape, lambda i: (0, 0)),
                  pl.BlockSpec(keys.shape, lambda i: (0, 0, 0, 0))],
        out_specs=[ospec] * 4,
        out_shape=[out] * 4,
        compiler_params=_params("arbitrary"),
        name="route",
    )(hb, wq_b, keys)


def _expert_kernel(hb_ref, u_ref, vt_ref, c_ref, bs_ref, wa_ref, eb_ref, o_ref, acc_ref):
    e = pl.program_id(1)
    groups = u_ref.shape[0] // PEER_KEYS

    @pl.when(e == 0)
    def _():
        acc_ref[...] = jnp.zeros_like(acc_ref)

    xu = _nt_dot(u_ref[...], hb_ref[...])
    act = 0.5 * xu * (1.0 + lax.erf(xu * (1.0 / math.sqrt(2.0))))
    parts = []
    for r in range(groups):
        a = e * groups + r
        g = None
        for h in range(PEER_HEADS):
            c = c_ref[h, pl.ds(a, 1), :]
            w = wa_ref[h, pl.ds(a, 1), :]
            term = jnp.where(bs_ref[h] >= c, eb_ref[h], 0.0) * w
            g = term if g is None else g + term
        parts.append(g)
    wgt = (jnp.concatenate(parts, axis=0) * act).astype(BF16)
    acc_ref[...] += jnp.dot(vt_ref[...], wgt, preferred_element_type=F32)

    @pl.when(e == pl.num_programs(1) - 1)
    def _():
        o_ref[...] = acc_ref[...].T


def _expert(hb, u_b, vt_b, c, bs, wa, eb):
    n, d = hb.shape
    ne = u_b.shape[0]
    tm = _tile(n, 512)
    te = _tile(ne, 512)
    rspec = pl.BlockSpec((PEER_HEADS, PEER_KEYS, tm), lambda i, e: (0, 0, i))
    return pl.pallas_call(
        _expert_kernel,
        grid=(n // tm, ne // te),
        in_specs=[pl.BlockSpec((tm, d), lambda i, e: (i, 0)),
                  pl.BlockSpec((te, d), lambda i, e: (e, 0)),
                  pl.BlockSpec((d, te), lambda i, e: (0, e)),
                  rspec, rspec, rspec, rspec],
        out_specs=pl.BlockSpec((tm, d), lambda i, e: (i, 0)),
        out_shape=jax.ShapeDtypeStruct((n, d), F32),
        scratch_shapes=[pltpu.VMEM((d, tm), F32)],
        compiler_params=_params("arbitrary", "arbitrary"),
        name="expert",
    )(hb, u_b, vt_b, c, bs, wa, eb)


def _final_kernel(alpha, h_ref, peer_ref, p_ref, gw_ref, gb_ref, pw_ref, g_ref, b_ref, y_ref):
    h2 = _layer_norm(alpha * h_ref[...] + peer_ref[...], g_ref[...], b_ref[...])
    gate = jax.nn.sigmoid(jnp.dot(h2.astype(BF16), gw_ref[...], preferred_element_type=F32) + gb_ref[...])
    emb = jnp.dot(p_ref[...].astype(BF16), pw_ref[...], preferred_element_type=F32)
    y_ref[...] = h2 + gate * emb


def _final(alpha, h, peer_out, p, gate_w_b, gate_b, ple_w_b, ln_g, ln_b):
    n, d = h.shape
    pd = p.shape[1]
    tm = _tile(n, 256)
    row = lambda i: (i, 0)
    full = lambda i: (0, 0)
    return pl.pallas_call(
        functools.partial(_final_kernel, alpha),
        grid=(n // tm,),
        in_specs=[pl.BlockSpec((tm, d), row), pl.BlockSpec((tm, d), row), pl.BlockSpec((tm, pd), row),
                  pl.BlockSpec((d, d), full), pl.BlockSpec((1, d), full), pl.BlockSpec((pd, d), full),
                  pl.BlockSpec((1, d), full), pl.BlockSpec((1, d), full)],
        out_specs=pl.BlockSpec((tm, d), row),
        out_shape=jax.ShapeDtypeStruct((n, d), F32),
        compiler_params=_params("arbitrary"),
        name="final",
    )(h, peer_out, p, gate_w_b, gate_b, ple_w_b, ln_g, ln_b)


def _channel_mix(alpha, x, p, attn, conv, lw):
    h, hb = _outproj(alpha, attn, conv, x, lw["w_out"], lw["ln1_g"], lw["ln1_b"])
    c, bs, wa, eb = _route(hb, lw["peer_wq"], lw["peer_keys"])
    peer_out = _expert(hb, lw["peer_u"], lw["peer_vt"], c, bs, wa, eb)
    return _final(alpha, h, peer_out, p, lw["gate_w"], lw["gate_b"], lw["ple_w"], lw["ln2_g"], lw["ln2_b"])


def kernel(x_prompt, x_sample, cache_k, cache_v, state_conv, page_table, p_prompt, p_sample, w_in, w_out, conv_w, conv_b, conv_ln_g, conv_ln_b, ln1_g, ln1_b, peer_wq, peer_keys, peer_u, peer_v, ln2_g, ln2_b, ple_w, gate_w, gate_b):
    depth = w_in.shape[0]
    batch, seq, d = x_prompt.shape
    nseq, dec_seq, _ = x_sample.shape
    c_conv = conv_w.shape[2]
    past_len = page_table.shape[1] * PAGE_SIZE
    alpha = (2.0 * depth) ** 0.25
    assert batch == 1 and w_in.shape[2] == 3 * ATTN_W + 2 * c_conv and c_conv == ATTN_W
    assert peer_keys.shape[1:] == (PEER_HEADS, 2, PEER_KEYS, PEER_HALF)

    cos_p, sin_p = _rotary_tables(jnp.arange(seq, dtype=jnp.int32))
    cos_s, sin_s = _rotary_tables(past_len + jnp.arange(dec_seq, dtype=jnp.int32))
    cos_s = jnp.tile(cos_s, (nseq, 1))
    sin_s = jnp.tile(sin_s, (nseq, 1))

    xp = x_prompt.reshape(seq, d)
    xs = x_sample.reshape(nseq * dec_seq, d)
    outs = [[] for _ in range(6)]
    for i in range(depth):
        vec = lambda a: a[i].reshape(1, -1)
        lw = dict(w_out=w_out[i].astype(BF16), ln1_g=vec(ln1_g), ln1_b=vec(ln1_b),
                  peer_wq=peer_wq[i].astype(BF16), peer_keys=peer_keys[i],
                  peer_u=peer_u[i].astype(BF16), peer_vt=peer_v[i].astype(BF16).T,
                  gate_w=gate_w[i].astype(BF16), gate_b=vec(gate_b), ple_w=ple_w[i].astype(BF16),
                  ln2_g=vec(ln2_g), ln2_b=vec(ln2_b))
        cw = (conv_w[i], vec(conv_b), vec(conv_ln_g), vec(conv_ln_b))
        w_in_b = w_in[i].astype(BF16)

        qp, kp, vp, up = _proj(xp, w_in_b, cos_p, sin_p)
        attn_p = _attn_prompt(qp, kp, vp)
        conv_p = _conv_prompt(up, *cw)
        xp_next = _channel_mix(alpha, xp, p_prompt[i].reshape(seq, -1), attn_p, conv_p, lw)

        qs, ks, vs, us = _proj(xs, w_in_b, cos_s, sin_s)
        attn_s = _attn_sample(i, qs, ks, vs, cache_k, cache_v, page_table)
        hist_s = jnp.concatenate([state_conv[i], us.reshape(nseq, dec_seq, c_conv)], axis=1)
        conv_s = _conv_sample(hist_s, *cw).reshape(nseq * dec_seq, c_conv)
        xs = _channel_mix(alpha, xs, p_sample[i].reshape(nseq * dec_seq, -1), attn_s, conv_s, lw)
        xp = xp_next

        hist_tail = jnp.concatenate([jnp.zeros((CONV_W - 1, c_conv), F32), up], axis=0)[-(CONV_W - 1):]
        outs[0].append(kp.reshape(batch, seq, N_HEADS, HEAD_DIM))
        outs[1].append(vp.reshape(batch, seq, N_HEADS, HEAD_DIM))
        outs[2].append(hist_tail.reshape(batch, CONV_W - 1, c_conv))
        outs[3].append(ks.reshape(nseq, dec_seq, N_HEADS, HEAD_DIM))
        outs[4].append(vs.reshape(nseq, dec_seq, N_HEADS, HEAD_DIM))
        outs[5].append(hist_s[:, -(CONV_W - 1):])
    kp_l, vp_l, cp_l, ks_l, vs_l, cs_l = [jnp.stack(o) for o in outs]
    return (xp.reshape(batch, seq, d), xs.reshape(nseq, dec_seq, d), kp_l, vp_l, cp_l, ks_l, vs_l, cs_l)
```

```python
import functools
import math

import jax
import jax.numpy as jnp
from jax import lax
from jax.experimental import pallas as pl
from jax.experimental.pallas import tpu as pltpu

N_HEADS = 8
HEAD_DIM = 128
ATTN_W = N_HEADS * HEAD_DIM
CONV_W = 31
MOBA_BLOCK = 256
MOBA_TOP_K = 3
ROPE_THETA = 10000.0
PAGE_SIZE = 128
PEER_HEADS = 8
PEER_KEYS = 128
PEER_TOPK = 16
PEER_HALF = 128
LN_EPS = 1e-5

LANES = 128
SUBLANES = 8
MXU_DEPTH = 256
VMEM_LIMIT = 56 * 1024 * 1024

NEG = -1e30
F32 = jnp.float32
BF16 = jnp.bfloat16
NT_DIMS = (((1,), (1,)), ((), ()))


def _nt_dot(a, b, precision=None):
    return lax.dot_general(a, b, NT_DIMS, precision=precision, preferred_element_type=F32)


def _layer_norm(x, g, b):
    mu = jnp.mean(x, axis=-1, keepdims=True)
    xc = x - mu
    var = jnp.mean(xc * xc, axis=-1, keepdims=True)
    return xc * lax.rsqrt(var + LN_EPS) * g + b


def _params(*sem):
    return pltpu.CompilerParams(dimension_semantics=sem, vmem_limit_bytes=VMEM_LIMIT)


def _tile(n, t):
    t = min(n, t)
    assert n % t == 0, (n, t)
    return t


def _proj_kernel(x_ref, w_ref, cos_ref, sin_ref, q_ref, k_ref, v_ref, u_ref, xb_ref, a_ref):
    j = pl.program_id(1)

    @pl.when(j == 0)
    def _():
        xb_ref[...] = x_ref[...].astype(BF16)

    z = jnp.dot(xb_ref[...], w_ref[...], preferred_element_type=F32)

    def rotary_to(o_ref):
        c = cos_ref[...]
        s = sin_ref[...]
        for h in range(N_HEADS):
            zh = z[:, h * HEAD_DIM:(h + 1) * HEAD_DIM]
            o_ref[:, h * HEAD_DIM:(h + 1) * HEAD_DIM] = zh * c + pltpu.roll(zh, HEAD_DIM // 2, 1) * s

    @pl.when(j == 0)
    def _():
        rotary_to(q_ref)

    @pl.when(j == 1)
    def _():
        rotary_to(k_ref)

    @pl.when(j == 2)
    def _():
        v_ref[...] = z

    @pl.when(j == 3)
    def _():
        a_ref[...] = z

    @pl.when(j == 4)
    def _():
        u_ref[...] = a_ref[...] * jax.nn.sigmoid(z)


def _proj(x, w_in_b, cos_t, sin_t):
    n, d = x.shape
    tm = _tile(n, 512)
    row = lambda i, j: (i, 0)
    out = jax.ShapeDtypeStruct((n, ATTN_W), F32)
    return pl.pallas_call(
        _proj_kernel,
        grid=(n // tm, 5),
        in_specs=[pl.BlockSpec((tm, d), row),
                  pl.BlockSpec((d, ATTN_W), lambda i, j: (0, j)),
                  pl.BlockSpec((tm, HEAD_DIM), row),
                  pl.BlockSpec((tm, HEAD_DIM), row)],
        out_specs=[pl.BlockSpec((tm, ATTN_W), row)] * 4,
        out_shape=[out] * 4,
        scratch_shapes=[pltpu.VMEM((tm, d), BF16), pltpu.VMEM((tm, ATTN_W), F32)],
        compiler_params=_params("arbitrary", "arbitrary"),
        name="proj",
    )(x, w_in_b, cos_t, sin_t)


def _rotary_tables(pos):
    half = HEAD_DIM // 2
    inv = ROPE_THETA ** (-jnp.arange(half, dtype=F32) * 2.0 / HEAD_DIM)
    ang = pos.astype(F32)[:, None] * inv[None, :]
    c, s = jnp.cos(ang), jnp.sin(ang)
    return jnp.concatenate([c, c], axis=1), jnp.concatenate([-s, s], axis=1)


EXP2_SCALE = HEAD_DIM ** -0.5 * math.log2(math.e)


def _attn_prompt_kernel(grp, q_ref, k_ref, v_ref, o_ref,
                        kaug_ref, vt_ref, kmean_ref, qaug_ref, m_ref, l_ref, acc_ref):
    i = pl.program_id(1)
    blk = MOBA_BLOCK
    nb = k_ref.shape[0] // blk
    nbp = -(-nb // SUBLANES) * SUBLANES

    @pl.when(i == 0)
    def _():
        kmean_ref[...] = jnp.zeros_like(kmean_ref)

        def fill(j, carry):
            r = pl.multiple_of(j * blk, blk)
            kb = k_ref[pl.ds(r, blk), :]
            kmean_ref[pl.ds(j, 1), :] = jnp.sum(kb, axis=0, keepdims=True) * (1.0 / blk)
            hot = lax.broadcasted_iota(jnp.int32, (blk, LANES), 1) == j
            kaug_ref[pl.ds(r, blk), 0:HEAD_DIM] = kb.astype(BF16)
            kaug_ref[pl.ds(r, blk), HEAD_DIM:HEAD_DIM + LANES] = hot.astype(F32).astype(BF16)
            vt_ref[:, pl.ds(r, blk)] = v_ref[pl.ds(r, blk), :].T.astype(BF16)
            return carry

        lax.fori_loop(0, nb, fill, 0)

    q = q_ref[...]
    gate = _nt_dot(kmean_ref[0:nbp, :], q, precision=lax.Precision.HIGHEST)
    row = lax.broadcasted_iota(jnp.int32, gate.shape, 0)
    past = row < i
    work = jnp.where(past, gate, -jnp.inf)
    sel = jnp.zeros(gate.shape, jnp.bool_)
    for _ in range(MOBA_TOP_K):
        mx = jnp.max(work, axis=0, keepdims=True)
        idx = jnp.min(jnp.where(work == mx, row, jnp.int32(1 << 30)), axis=0, keepdims=True)
        pick = row == idx
        sel = jnp.logical_or(sel, pick)
        work = jnp.where(pick, -jnp.inf, work)
    keep = jnp.logical_or(jnp.logical_and(sel, past), row == i)
    pen = jnp.where(keep, 0.0, NEG)
    pen = jnp.concatenate([pen, jnp.zeros((LANES - nbp, blk), F32)], axis=0)
    qaug_ref[0:HEAD_DIM, :] = q.T.astype(BF16)
    qaug_ref[HEAD_DIM:HEAD_DIM + LANES, :] = pen.astype(BF16)

    gk = grp * blk
    gi = i // grp

    def scores(g):
        r = pl.multiple_of(g * gk, gk)
        s = jnp.dot(kaug_ref[pl.ds(r, gk), :], qaug_ref[...], preferred_element_type=F32)
        return s, r

    s, r = scores(gi)
    key_pos = lax.broadcasted_iota(jnp.int32, s.shape, 0) + gi * gk
    qry_pos = lax.broadcasted_iota(jnp.int32, s.shape, 1) + i * blk
    s = jnp.where(key_pos <= qry_pos, s, NEG)
    m0 = jnp.max(s, axis=0, keepdims=True)
    p = jnp.exp2((s - m0) * EXP2_SCALE)
    m_ref[...] = m0
    l_ref[...] = jnp.sum(p, axis=0, keepdims=True)
    acc_ref[...] = jnp.dot(vt_ref[:, pl.ds(r, gk)], p.astype(BF16), preferred_element_type=F32)

    def body(g, carry):
        s, r = scores(g)
        m_old = m_ref[...]
        m_new = jnp.maximum(m_old, jnp.max(s, axis=0, keepdims=True))
        alpha = jnp.exp2((m_old - m_new) * EXP2_SCALE)
        p = jnp.exp2((s - m_new) * EXP2_SCALE)
        l_ref[...] = alpha * l_ref[...] + jnp.sum(p, axis=0, keepdims=True)
        acc_ref[...] = alpha * acc_ref[...] + jnp.dot(
            vt_ref[:, pl.ds(r, gk)], p.astype(BF16), preferred_element_type=F32)
        m_ref[...] = m_new
        return carry

    lax.fori_loop(0, gi, body, 0)
    o_ref[...] = (acc_ref[...] / l_ref[...]).T.astype(o_ref.dtype)


def _attn_prompt(q, k, v):
    s = q.shape[0]
    blk = MOBA_BLOCK
    nb = s // blk
    assert s % blk == 0 and nb <= LANES
    grp = next(g for g in (4, 2, 1) if nb % g == 0)
    qspec = pl.BlockSpec((blk, HEAD_DIM), lambda h, i: (i, h))
    kvspec = pl.BlockSpec((s, HEAD_DIM), lambda h, i: (0, h))
    return pl.pallas_call(
        functools.partial(_attn_prompt_kernel, grp),
        grid=(N_HEADS, nb),
        in_specs=[qspec, kvspec, kvspec],
        out_specs=qspec,
        out_shape=jax.ShapeDtypeStruct((s, ATTN_W), BF16),
        scratch_shapes=[pltpu.VMEM((s, HEAD_DIM + LANES), BF16),
                        pltpu.VMEM((HEAD_DIM, s), BF16),
                        pltpu.VMEM((LANES, HEAD_DIM), F32),
                        pltpu.VMEM((HEAD_DIM + LANES, blk), BF16),
                        pltpu.VMEM((1, blk), F32),
                        pltpu.VMEM((1, blk), F32),
                        pltpu.VMEM((HEAD_DIM, blk), F32)],
        compiler_params=_params("arbitrary", "arbitrary"),
        name="attn_prompt",
    )(q, k, v)


def _attn_sample_kernel(pt_ref, q_ref, kn_ref, vn_ref, ke_ref, ko_ref, ve_ref, vo_ref, o_ref,
                        g_ref, m_ref, l_ref, acc_ref):
    del pt_ref
    j = pl.program_id(1)
    nblk = pl.num_programs(1)
    t = q_ref.shape[0]
    scale = HEAD_DIM ** -0.5
    heads = range(N_HEADS)
    q = q_ref[...]
    qh = [q[:, h * HEAD_DIM:(h + 1) * HEAD_DIM] for h in heads]

    def attend(k_heads, v_heads, mask):
        s = jnp.concatenate([_nt_dot(qh[h].astype(BF16), k_heads[h].astype(BF16)) for h in heads], axis=0)
        s = s * scale
        if mask is not None:
            s = jnp.where(mask, s, NEG)
        m = jnp.max(s, axis=1, keepdims=True)
        p = jnp.exp(s - m)
        l = jnp.sum(p, axis=1, keepdims=True)
        acc = jnp.concatenate([
            jnp.dot(p[h * t:(h + 1) * t].astype(BF16), v_heads[h].astype(BF16), preferred_element_type=F32)
            for h in heads], axis=0)
        return m, l, acc

    def paged(even_ref, odd_ref):
        return [jnp.concatenate([r[0, 0, pl.ds(h, PAGE_SIZE, stride=N_HEADS), :] for r in (even_ref, odd_ref)],
                                axis=0) for h in heads]

    k_heads = paged(ke_ref, ko_ref)
    g_ref[j] = jnp.concatenate([
        jnp.sum(qh[h] * (jnp.sum(k_heads[h], axis=0, keepdims=True) * (1.0 / MOBA_BLOCK)), axis=1, keepdims=True)
        for h in heads], axis=0)
    m, l, acc = attend(k_heads, paged(ve_ref, vo_ref), None)
    m_ref[j] = m
    l_ref[j] = l
    acc_ref[j] = acc

    @pl.when(j == nblk - 1)
    def _():
        pad = jnp.zeros((PAGE_SIZE - t, HEAD_DIM), F32)
        kn = [jnp.concatenate([kn_ref[:, h * HEAD_DIM:(h + 1) * HEAD_DIM], pad], axis=0) for h in heads]
        vn = [jnp.concatenate([vn_ref[:, h * HEAD_DIM:(h + 1) * HEAD_DIM], pad], axis=0) for h in heads]
        shape = (N_HEADS * t, PAGE_SIZE)
        tok = lax.broadcasted_iota(jnp.int32, shape, 0) % t
        key = lax.broadcasted_iota(jnp.int32, shape, 1)
        m_own, l_own, acc_own = attend(kn, vn, key <= tok)

        n = g_ref.shape[0]
        gs = [g_ref[a] for a in range(n)]
        valid = []
        for a in range(n):
            rank = jnp.zeros(gs[a].shape, jnp.int32)
            for b in range(n):
                if b == a:
                    continue
                ahead = gs[b] > gs[a]
                if b < a:
                    ahead = jnp.logical_or(ahead, gs[b] == gs[a])
                rank = rank + ahead.astype(jnp.int32)
            valid.append(rank < MOBA_TOP_K)
        m_tot = m_own
        for a in range(n):
            m_tot = jnp.maximum(m_tot, jnp.where(valid[a], m_ref[a], NEG))
        w = jnp.exp(m_own - m_tot)
        l_tot = w * l_own
        acc = w * acc_own
        for a in range(n):
            w = jnp.where(valid[a], jnp.exp(m_ref[a] - m_tot), 0.0)
            l_tot = l_tot + w * l_ref[a]
            acc = acc + w * acc_ref[a]
        out = acc / l_tot
        for h in heads:
            o_ref[:, h * HEAD_DIM:(h + 1) * HEAD_DIM] = out[h * t:(h + 1) * t, :]


def _attn_sample(layer, q, k_new, v_new, cache_k, cache_v, page_table):
    nseq, npages = page_table.shape
    t = q.shape[0] // nseq
    ppb = MOBA_BLOCK // PAGE_SIZE
    assert ppb == 2 and npages % ppb == 0 and t % SUBLANES == 0 and t <= PAGE_SIZE
    assert (npages * PAGE_SIZE) // MOBA_BLOCK == (npages * PAGE_SIZE + t - 1) // MOBA_BLOCK
    assert cache_k.shape[2:] == (PAGE_SIZE, N_HEADS, HEAD_DIM)
    nblk = npages // ppb
    pt = page_table.reshape(-1).astype(jnp.int32)
    page_rows = PAGE_SIZE * N_HEADS
    cache_k = cache_k.reshape(cache_k.shape[0], cache_k.shape[1], page_rows, HEAD_DIM)
    cache_v = cache_v.reshape(cache_v.shape[0], cache_v.shape[1], page_rows, HEAD_DIM)

    new = pl.BlockSpec((t, ATTN_W), lambda b, j, pt: (b, 0))

    def page(off):
        return pl.BlockSpec((1, 1, page_rows, HEAD_DIM),
                            lambda b, j, pt: (layer, pt[b * npages + ppb * j + off], 0, 0))

    rows = N_HEADS * t
    return pl.pallas_call(
        _attn_sample_kernel,
        grid_spec=pltpu.PrefetchScalarGridSpec(
            num_scalar_prefetch=1,
            grid=(nseq, nblk),
            in_specs=[new, new, new, page(0), page(1), page(0), page(1)],
            out_specs=new,
            scratch_shapes=[pltpu.VMEM((nblk, rows, 1), F32),
                            pltpu.VMEM((nblk, rows, 1), F32),
                            pltpu.VMEM((nblk, rows, 1), F32),
                            pltpu.VMEM((nblk, rows, HEAD_DIM), F32)]),
        out_shape=jax.ShapeDtypeStruct((nseq * t, ATTN_W), F32),
        compiler_params=_params("arbitrary", "arbitrary"),
        name="attn_sample",
    )(pt, q, k_new, v_new, cache_k, cache_k, cache_v, cache_v)


CONV_HALO = 32
CONV_CHUNK = 32


def _conv_finish(y, b_ref, g_ref, beta_ref):
    y = _layer_norm(y + b_ref[...], g_ref[...], beta_ref[...])
    return y * jax.nn.sigmoid(y)


def _conv_prompt_kernel(u_ref, halo_ref, w_ref, b_ref, g_ref, beta_ref, o_ref, buf_ref):
    i = pl.program_id(0)
    tm = u_ref.shape[0]
    buf_ref[0:CONV_HALO, :] = jnp.where(i == 0, 0.0, halo_ref[...])
    buf_ref[CONV_HALO:CONV_HALO + tm, :] = u_ref[...]
    lead = CONV_HALO - (CONV_W - 1)
    for c in range(tm // CONV_CHUNK):
        base = c * CONV_CHUNK + lead
        acc = w_ref[0:1, :] * buf_ref[base:base + CONV_CHUNK, :]
        for k in range(1, CONV_W):
            acc = acc + w_ref[k:k + 1, :] * buf_ref[base + k:base + k + CONV_CHUNK, :]
        o_ref[c * CONV_CHUNK:(c + 1) * CONV_CHUNK, :] = _conv_finish(
            acc, b_ref, g_ref, beta_ref).astype(o_ref.dtype)


def _conv_prompt(u, conv_w, conv_b, ln_g, ln_b):
    s, c = u.shape
    tm = _tile(s, 256)
    assert tm % CONV_HALO == 0 and tm % CONV_CHUNK == 0
    per = tm // CONV_HALO
    full = lambda i: (0, 0)
    return pl.pallas_call(
        _conv_prompt_kernel,
        grid=(s // tm,),
        in_specs=[pl.BlockSpec((tm, c), lambda i: (i, 0)),
                  pl.BlockSpec((CONV_HALO, c), lambda i: (jnp.maximum(i * per - 1, 0), 0)),
                  pl.BlockSpec((CONV_W, c), full),
                  pl.BlockSpec((1, c), full), pl.BlockSpec((1, c), full), pl.BlockSpec((1, c), full)],
        out_specs=pl.BlockSpec((tm, c), lambda i: (i, 0)),
        out_shape=jax.ShapeDtypeStruct((s, c), BF16),
        scratch_shapes=[pltpu.VMEM((CONV_HALO + tm, c), F32)],
        compiler_params=_params("arbitrary"),
        name="conv_prompt",
    )(u, u, conv_w, conv_b, ln_g, ln_b)


def _conv_sample_kernel(hist_ref, w_ref, b_ref, g_ref, beta_ref, o_ref):
    t = o_ref.shape[1]
    for s in range(hist_ref.shape[0]):
        acc = w_ref[0:1, :] * hist_ref[s, 0:t, :]
        for k in range(1, CONV_W):
            acc = acc + w_ref[k:k + 1, :] * hist_ref[s, k:k + t, :]
        o_ref[s] = _conv_finish(acc, b_ref, g_ref, beta_ref)


def _conv_sample(hist, conv_w, conv_b, ln_g, ln_b):
    nseq, rows, c = hist.shape
    t = rows - (CONV_W - 1)
    gs = _tile(nseq, 8)
    full = lambda i: (0, 0)
    return pl.pallas_call(
        _conv_sample_kernel,
        grid=(nseq // gs,),
        in_specs=[pl.BlockSpec((gs, rows, c), lambda i: (i, 0, 0)),
                  pl.BlockSpec((CONV_W, c), full),
                  pl.BlockSpec((1, c), full), pl.BlockSpec((1, c), full), pl.BlockSpec((1, c), full)],
        out_specs=pl.BlockSpec((gs, t, c), lambda i: (i, 0, 0)),
        out_shape=jax.ShapeDtypeStruct((nseq, t, c), F32),
        compiler_params=_params("arbitrary"),
        name="conv_sample",
    )(hist, conv_w, conv_b, ln_g, ln_b)


def _outproj_kernel(alpha, attn_ref, conv_ref, x_ref, wa_ref, wc_ref, g_ref, b_ref, h_ref, hb_ref):
    mixed = jnp.dot(attn_ref[...].astype(BF16), wa_ref[...], preferred_element_type=F32)
    mixed = mixed + jnp.dot(conv_ref[...].astype(BF16), wc_ref[...], preferred_element_type=F32)
    h = _layer_norm(alpha * x_ref[...] + mixed, g_ref[...], b_ref[...])
    h_ref[...] = h
    hb_ref[...] = h.astype(BF16)


def _outproj(alpha, attn, conv, x, w_out_b, ln_g, ln_b):
    n, d = x.shape
    wa = attn.shape[1]
    wc = conv.shape[1]
    tm = _tile(n, 256)
    row = lambda i: (i, 0)
    full = lambda i: (0, 0)
    return pl.pallas_call(
        functools.partial(_outproj_kernel, alpha),
        grid=(n // tm,),
        in_specs=[pl.BlockSpec((tm, wa), row), pl.BlockSpec((tm, wc), row), pl.BlockSpec((tm, d), row),
                  pl.BlockSpec((wa, d), full), pl.BlockSpec((wc, d), lambda i: (wa // wc, 0)),
                  pl.BlockSpec((1, d), full), pl.BlockSpec((1, d), full)],
        out_specs=[pl.BlockSpec((tm, d), row)] * 2,
        out_shape=[jax.ShapeDtypeStruct((n, d), F32), jax.ShapeDtypeStruct((n, d), BF16)],
        compiler_params=_params("arbitrary"),
        name="outproj",
    )(attn, conv, x, w_out_b, w_out_b, ln_g, ln_b)


def _top_sorted(x, k):
    out = []
    for _ in range(k):
        mx = jnp.max(x, axis=0, keepdims=True)
        out.append(mx)
        x = jnp.where(x >= mx, -jnp.inf, x)
    return out


def _route_kernel(hb_ref, wq_ref, keys_ref, c_ref, bs_ref, wa_ref, eb_ref):
    q = jnp.dot(hb_ref[...], wq_ref[...], preferred_element_type=F32)
    k = PEER_TOPK
    for h in range(PEER_HEADS):
        lo = h * 2 * PEER_HALF
        sa = _nt_dot(keys_ref[h, 0], q[:, lo:lo + PEER_HALF], precision=lax.Precision.HIGHEST)
        sb = _nt_dot(keys_ref[h, 1], q[:, lo + PEER_HALF:lo + 2 * PEER_HALF],
                     precision=lax.Precision.HIGHEST)
        ta = _top_sorted(sa, k + 1)
        tb_rows = _top_sorted(sb, k + 1)
        tb = jnp.concatenate(tb_rows[:k], axis=0)
        half = k // 2
        cand = [ta[0] + tb]
        cand += [ta[i] + tb[0:half] for i in range(1, half)]
        cand += [jnp.concatenate(ta[half:k], axis=0) + tb[0:1]]
        row = lax.broadcasted_iota(jnp.int32, (SUBLANES, sa.shape[1]), 0)
        cand += [jnp.where(row == 0, ta[k] + tb_rows[0],
                           jnp.where(row == 1, ta[0] + tb_rows[k], -jnp.inf))]
        cand = jnp.concatenate(cand, axis=0)
        best = _top_sorted(cand, k + 1)
        tau = 0.5 * (best[k - 1] + best[k])
        top = ta[0] + tb[0:1]
        z = jnp.sum(jnp.where(cand >= tau, jnp.exp(cand - top), 0.0), axis=0, keepdims=True)
        c_ref[h] = tau - sa
        bs_ref[h] = sb
        wa_ref[h] = jnp.exp(sa - ta[0]) / z
        eb_ref[h] = jnp.exp(sb - tb[0:1])


def _route(hb, wq_b, keys):
    n, d = hb.shape
    tm = _tile(n, 256)
    out = jax.ShapeDtypeStruct((PEER_HEADS, PEER_KEYS, n), F32)
    ospec = pl.BlockSpec((PEER_HEADS, PEER_KEYS, tm), lambda i: (0, 0, i))
    return pl.pallas_call(
        _route_kernel,
        grid=(n // tm,),
        in_specs=[pl.BlockSpec((tm, d), lambda i: (i, 0)),
                  pl.BlockSpec(wq_b.shape, lambda i: (0, 0)),
                  pl.BlockSpec(keys.shape, lambda i: (0, 0, 0, 0))],
        out_specs=[ospec] * 4,
        out_shape=[out] * 4,
        compiler_params=_params("arbitrary"),
        name="route",
    )(hb, wq_b, keys)


def _expert_kernel(hb_ref, u_ref, vt_ref, c_ref, bs_ref, wa_ref, eb_ref, o_ref, acc_ref, xu_ref, wgt_ref):
    e = pl.program_id(1)
    nblocks = pl.num_programs(1) - 2
    groups = u_ref.shape[0] // PEER_KEYS
    cur = e % 2
    prev = 1 - cur

    @pl.when(e == 0)
    def _():
        acc_ref[...] = jnp.zeros_like(acc_ref)
        xu_ref[1] = jnp.zeros(xu_ref.shape[1:], F32)
        wgt_ref[1] = jnp.zeros(wgt_ref.shape[1:], BF16)

    acc_ref[...] += jnp.dot(vt_ref[...], wgt_ref[prev], preferred_element_type=F32)

    block = jnp.clip(e - 1, 0, nblocks - 1)
    xu = xu_ref[prev]
    act = 0.5 * xu * (1.0 + lax.erf(xu * (1.0 / math.sqrt(2.0))))
    for r in range(groups):
        a = block * groups + r
        g = None
        for h in range(PEER_HEADS):
            c = c_ref[h, pl.ds(a, 1), :]
            w = wa_ref[h, pl.ds(a, 1), :]
            term = jnp.where(bs_ref[h] >= c, eb_ref[h], 0.0) * w
            g = term if g is None else g + term
        rows = slice(r * PEER_KEYS, (r + 1) * PEER_KEYS)
        wgt_ref[cur, rows, :] = (g * act[rows]).astype(BF16)

    xu_ref[cur] = _nt_dot(u_ref[...], hb_ref[...])

    @pl.when(e == pl.num_programs(1) - 1)
    def _():
        o_ref[...] = acc_ref[...].T


def _expert(hb, u_b, vt_b, c, bs, wa, eb):
    n, d = hb.shape
    ne = u_b.shape[0]
    tm = _tile(n, 512)
    te = _tile(ne, 512)
    nblocks = ne // te
    rspec = pl.BlockSpec((PEER_HEADS, PEER_KEYS, tm), lambda i, e: (0, 0, i))
    return pl.pallas_call(
        _expert_kernel,
        grid=(n // tm, nblocks + 2),
        in_specs=[pl.BlockSpec((tm, d), lambda i, e: (i, 0)),
                  pl.BlockSpec((te, d), lambda i, e: (jnp.minimum(e, nblocks - 1), 0)),
                  pl.BlockSpec((d, te), lambda i, e: (0, jnp.maximum(e - 2, 0))),
                  rspec, rspec, rspec, rspec],
        out_specs=pl.BlockSpec((tm, d), lambda i, e: (i, 0)),
        out_shape=jax.ShapeDtypeStruct((n, d), F32),
        scratch_shapes=[pltpu.VMEM((d, tm), F32), pltpu.VMEM((2, te, tm), F32), pltpu.VMEM((2, te, tm), BF16)],
        compiler_params=_params("arbitrary", "arbitrary"),
        name="expert",
    )(hb, u_b, vt_b, c, bs, wa, eb)


def _final_kernel(alpha, h_ref, peer_ref, p_ref, gw_ref, gb_ref, pw_ref, g_ref, b_ref, y_ref):
    h2 = _layer_norm(alpha * h_ref[...] + peer_ref[...], g_ref[...], b_ref[...])
    gate = jax.nn.sigmoid(jnp.dot(h2.astype(BF16), gw_ref[...], preferred_element_type=F32) + gb_ref[...])
    emb = jnp.dot(p_ref[...].astype(BF16), pw_ref[...], preferred_element_type=F32)
    y_ref[...] = h2 + gate * emb


def _final(alpha, h, peer_out, p, gate_w_b, gate_b, ple_w_b, ln_g, ln_b):
    n, d = h.shape
    pd = p.shape[1]
    tm = _tile(n, 256)
    row = lambda i: (i, 0)
    full = lambda i: (0, 0)
    return pl.pallas_call(
        functools.partial(_final_kernel, alpha),
        grid=(n // tm,),
        in_specs=[pl.BlockSpec((tm, d), row), pl.BlockSpec((tm, d), row), pl.BlockSpec((tm, pd), row),
                  pl.BlockSpec((d, d), full), pl.BlockSpec((1, d), full), pl.BlockSpec((pd, d), full),
                  pl.BlockSpec((1, d), full), pl.BlockSpec((1, d), full)],
        out_specs=pl.BlockSpec((tm, d), row),
        out_shape=jax.ShapeDtypeStruct((n, d), F32),
        compiler_params=_params("arbitrary"),
        name="final",
    )(h, peer_out, p, gate_w_b, gate_b, ple_w_b, ln_g, ln_b)


def _channel_mix(alpha, x, p, attn, conv, lw):
    h, hb = _outproj(alpha, attn, conv, x, lw["w_out"], lw["ln1_g"], lw["ln1_b"])
    c, bs, wa, eb = _route(hb, lw["peer_wq"], lw["peer_keys"])
    peer_out = _expert(hb, lw["peer_u"], lw["peer_vt"], c, bs, wa, eb)
    return _final(alpha, h, peer_out, p, lw["gate_w"], lw["gate_b"], lw["ple_w"], lw["ln2_g"], lw["ln2_b"])


def kernel(x_prompt, x_sample, cache_k, cache_v, state_conv, page_table, p_prompt, p_sample, w_in, w_out, conv_w, conv_b, conv_ln_g, conv_ln_b, ln1_g, ln1_b, peer_wq, peer_keys, peer_u, peer_v, ln2_g, ln2_b, ple_w, gate_w, gate_b):
    depth = w_in.shape[0]
    batch, seq, d = x_prompt.shape
    nseq, dec_seq, _ = x_sample.shape
    c_conv = conv_w.shape[2]
    past_len = page_table.shape[1] * PAGE_SIZE
    alpha = (2.0 * depth) ** 0.25
    assert batch == 1 and w_in.shape[2] == 3 * ATTN_W + 2 * c_conv and c_conv == ATTN_W
    assert peer_keys.shape[1:] == (PEER_HEADS, 2, PEER_KEYS, PEER_HALF)

    cos_p, sin_p = _rotary_tables(jnp.arange(seq, dtype=jnp.int32))
    cos_s, sin_s = _rotary_tables(past_len + jnp.arange(dec_seq, dtype=jnp.int32))
    cos_s = jnp.tile(cos_s, (nseq, 1))
    sin_s = jnp.tile(sin_s, (nseq, 1))

    xp = x_prompt.reshape(seq, d)
    xs = x_sample.reshape(nseq * dec_seq, d)
    outs = [[] for _ in range(6)]
    for i in range(depth):
        vec = lambda a: a[i].reshape(1, -1)
        lw = dict(w_out=w_out[i].astype(BF16), ln1_g=vec(ln1_g), ln1_b=vec(ln1_b),
                  peer_wq=peer_wq[i].astype(BF16), peer_keys=peer_keys[i],
                  peer_u=peer_u[i].astype(BF16), peer_vt=peer_v[i].astype(BF16).T,
                  gate_w=gate_w[i].astype(BF16), gate_b=vec(gate_b), ple_w=ple_w[i].astype(BF16),
                  ln2_g=vec(ln2_g), ln2_b=vec(ln2_b))
        cw = (conv_w[i], vec(conv_b), vec(conv_ln_g), vec(conv_ln_b))
        w_in_b = w_in[i].astype(BF16)

        qp, kp, vp, up = _proj(xp, w_in_b, cos_p, sin_p)
        attn_p = _attn_prompt(qp, kp, vp)
        conv_p = _conv_prompt(up, *cw)
        xp_next = _channel_mix(alpha, xp, p_prompt[i].reshape(seq, -1), attn_p, conv_p, lw)

        qs, ks, vs, us = _proj(xs, w_in_b, cos_s, sin_s)
        attn_s = _attn_sample(i, qs, ks, vs, cache_k, cache_v, page_table)
        hist_s = jnp.concatenate([state_conv[i], us.reshape(nseq, dec_seq, c_conv)], axis=1)
        conv_s = _conv_sample(hist_s, *cw).reshape(nseq * dec_seq, c_conv)
        xs = _channel_mix(alpha, xs, p_sample[i].reshape(nseq * dec_seq, -1), attn_s, conv_s, lw)
        xp = xp_next

        hist_tail = jnp.concatenate([jnp.zeros((CONV_W - 1, c_conv), F32), up], axis=0)[-(CONV_W - 1):]
        outs[0].append(kp.reshape(batch, seq, N_HEADS, HEAD_DIM))
        outs[1].append(vp.reshape(batch, seq, N_HEADS, HEAD_DIM))
        outs[2].append(hist_tail.reshape(batch, CONV_W - 1, c_conv))
        outs[3].append(ks.reshape(nseq, dec_seq, N_HEADS, HEAD_DIM))
        outs[4].append(vs.reshape(nseq, dec_seq, N_HEADS, HEAD_DIM))
        outs[5].append(hist_s[:, -(CONV_W - 1):])
    kp_l, vp_l, cp_l, ks_l, vs_l, cs_l = [jnp.stack(o) for o in outs]
    return (xp.reshape(batch, seq, d), xs.reshape(nseq, dec_seq, d), kp_l, vp_l, cp_l, ks_l, vs_l, cs_l)
```

```python
import functools
import math

import jax
import jax.numpy as jnp
from jax import lax
from jax.experimental import pallas as pl
from jax.experimental.pallas import tpu as pltpu

N_HEADS = 8
HEAD_DIM = 128
ATTN_W = N_HEADS * HEAD_DIM
CONV_W = 31
MOBA_BLOCK = 256
MOBA_TOP_K = 3
ROPE_THETA = 10000.0
PAGE_SIZE = 128
PEER_HEADS = 8
PEER_KEYS = 128
PEER_TOPK = 16
PEER_HALF = 128
LN_EPS = 1e-5

LANES = 128
SUBLANES = 8
MXU_DEPTH = 256
VMEM_LIMIT = 56 * 1024 * 1024

NEG = -1e30
F32 = jnp.float32
BF16 = jnp.bfloat16
NT_DIMS = (((1,), (1,)), ((), ()))


def _nt_dot(a, b, precision=None):
    return lax.dot_general(a, b, NT_DIMS, precision=precision, preferred_element_type=F32)


def _layer_norm(x, g, b):
    mu = jnp.mean(x, axis=-1, keepdims=True)
    xc = x - mu
    var = jnp.mean(xc * xc, axis=-1, keepdims=True)
    return xc * lax.rsqrt(var + LN_EPS) * g + b


def _params(*sem):
    return pltpu.CompilerParams(dimension_semantics=sem, vmem_limit_bytes=VMEM_LIMIT)


def _tile(n, t):
    t = min(n, t)
    assert n % t == 0, (n, t)
    return t


def _proj_kernel(x_ref, w_ref, cos_ref, sin_ref, q_ref, k_ref, v_ref, u_ref, xb_ref, a_ref):
    j = pl.program_id(1)

    @pl.when(j == 0)
    def _():
        xb_ref[...] = x_ref[...].astype(BF16)

    z = jnp.dot(xb_ref[...], w_ref[...], preferred_element_type=F32)

    def rotary_to(o_ref):
        c = cos_ref[...]
        s = sin_ref[...]
        for h in range(N_HEADS):
            zh = z[:, h * HEAD_DIM:(h + 1) * HEAD_DIM]
            o_ref[:, h * HEAD_DIM:(h + 1) * HEAD_DIM] = zh * c + pltpu.roll(zh, HEAD_DIM // 2, 1) * s

    @pl.when(j == 0)
    def _():
        rotary_to(q_ref)

    @pl.when(j == 1)
    def _():
        rotary_to(k_ref)

    @pl.when(j == 2)
    def _():
        v_ref[...] = z

    @pl.when(j == 3)
    def _():
        a_ref[...] = z

    @pl.when(j == 4)
    def _():
        u_ref[...] = a_ref[...] * jax.nn.sigmoid(z)


def _proj(x, w_in_b, cos_t, sin_t):
    n, d = x.shape
    tm = _tile(n, 512)
    row = lambda i, j: (i, 0)
    out = jax.ShapeDtypeStruct((n, ATTN_W), F32)
    return pl.pallas_call(
        _proj_kernel,
        grid=(n // tm, 5),
        in_specs=[pl.BlockSpec((tm, d), row),
                  pl.BlockSpec((d, ATTN_W), lambda i, j: (0, j)),
                  pl.BlockSpec((tm, HEAD_DIM), row),
                  pl.BlockSpec((tm, HEAD_DIM), row)],
        out_specs=[pl.BlockSpec((tm, ATTN_W), row)] * 4,
        out_shape=[out] * 4,
        scratch_shapes=[pltpu.VMEM((tm, d), BF16), pltpu.VMEM((tm, ATTN_W), F32)],
        compiler_params=_params("arbitrary", "arbitrary"),
        name="proj",
    )(x, w_in_b, cos_t, sin_t)


def _rotary_tables(pos):
    half = HEAD_DIM // 2
    inv = ROPE_THETA ** (-jnp.arange(half, dtype=F32) * 2.0 / HEAD_DIM)
    ang = pos.astype(F32)[:, None] * inv[None, :]
    c, s = jnp.cos(ang), jnp.sin(ang)
    return jnp.concatenate([c, c], axis=1), jnp.concatenate([-s, s], axis=1)


EXP2_SCALE = HEAD_DIM ** -0.5 * math.log2(math.e)


def _attn_prompt_kernel(grp, q_ref, k_ref, v_ref, o_ref,
                        kaug_ref, vt_ref, kmean_ref, qaug_ref, m_ref, l_ref, acc_ref):
    i = pl.program_id(1)
    blk = MOBA_BLOCK
    nb = k_ref.shape[0] // blk
    nbp = -(-nb // SUBLANES) * SUBLANES

    @pl.when(i == 0)
    def _():
        kmean_ref[...] = jnp.zeros_like(kmean_ref)

        def fill(j, carry):
            r = pl.multiple_of(j * blk, blk)
            kb = k_ref[pl.ds(r, blk), :]
            kmean_ref[pl.ds(j, 1), :] = jnp.sum(kb, axis=0, keepdims=True) * (1.0 / blk)
            hot = lax.broadcasted_iota(jnp.int32, (blk, LANES), 1) == j
            kaug_ref[pl.ds(r, blk), 0:HEAD_DIM] = kb.astype(BF16)
            kaug_ref[pl.ds(r, blk), HEAD_DIM:HEAD_DIM + LANES] = hot.astype(F32).astype(BF16)
            vt_ref[:, pl.ds(r, blk)] = v_ref[pl.ds(r, blk), :].T.astype(BF16)
            return carry

        lax.fori_loop(0, nb, fill, 0)

    q = q_ref[...]
    gate = _nt_dot(kmean_ref[0:nbp, :], q, precision=lax.Precision.HIGHEST)
    row = lax.broadcasted_iota(jnp.int32, gate.shape, 0)
    past = row < i
    work = jnp.where(past, gate, -jnp.inf)
    sel = jnp.zeros(gate.shape, jnp.bool_)
    for _ in range(MOBA_TOP_K):
        mx = jnp.max(work, axis=0, keepdims=True)
        idx = jnp.min(jnp.where(work == mx, row, jnp.int32(1 << 30)), axis=0, keepdims=True)
        pick = row == idx
        sel = jnp.logical_or(sel, pick)
        work = jnp.where(pick, -jnp.inf, work)
    keep = jnp.logical_or(jnp.logical_and(sel, past), row == i)
    pen = jnp.where(keep, 0.0, NEG)
    pen = jnp.concatenate([pen, jnp.zeros((LANES - nbp, blk), F32)], axis=0)
    qaug_ref[0:HEAD_DIM, :] = q.T.astype(BF16)
    qaug_ref[HEAD_DIM:HEAD_DIM + LANES, :] = pen.astype(BF16)

    gk = grp * blk
    gi = i // grp

    def scores(g):
        r = pl.multiple_of(g * gk, gk)
        s = jnp.dot(kaug_ref[pl.ds(r, gk), :], qaug_ref[...], preferred_element_type=F32)
        return s, r

    s, r = scores(gi)
    key_pos = lax.broadcasted_iota(jnp.int32, s.shape, 0) + gi * gk
    qry_pos = lax.broadcasted_iota(jnp.int32, s.shape, 1) + i * blk
    s = jnp.where(key_pos <= qry_pos, s, NEG)
    m0 = jnp.max(s, axis=0, keepdims=True)
    p = jnp.exp2((s - m0) * EXP2_SCALE)
    m_ref[...] = m0
    l_ref[...] = jnp.sum(p, axis=0, keepdims=True)
    acc_ref[...] = jnp.dot(vt_ref[:, pl.ds(r, gk)], p.astype(BF16), preferred_element_type=F32)

    def body(g, carry):
        s, r = scores(g)
        m_old = m_ref[...]
        m_new = jnp.maximum(m_old, jnp.max(s, axis=0, keepdims=True))
        alpha = jnp.exp2((m_old - m_new) * EXP2_SCALE)
        p = jnp.exp2((s - m_new) * EXP2_SCALE)
        l_ref[...] = alpha * l_ref[...] + jnp.sum(p, axis=0, keepdims=True)
        acc_ref[...] = alpha * acc_ref[...] + jnp.dot(
            vt_ref[:, pl.ds(r, gk)], p.astype(BF16), preferred_element_type=F32)
        m_ref[...] = m_new
        return carry

    lax.fori_loop(0, gi, body, 0)
    o_ref[...] = (acc_ref[...] / l_ref[...]).T.astype(o_ref.dtype)


def _attn_prompt(q, k, v):
    s = q.shape[0]
    blk = MOBA_BLOCK
    nb = s // blk
    assert s % blk == 0 and nb <= LANES
    grp = next(g for g in (4, 2, 1) if nb % g == 0)
    qspec = pl.BlockSpec((blk, HEAD_DIM), lambda h, i: (i, h))
    kvspec = pl.BlockSpec((s, HEAD_DIM), lambda h, i: (0, h))
    return pl.pallas_call(
        functools.partial(_attn_prompt_kernel, grp),
        grid=(N_HEADS, nb),
        in_specs=[qspec, kvspec, kvspec],
        out_specs=qspec,
        out_shape=jax.ShapeDtypeStruct((s, ATTN_W), BF16),
        scratch_shapes=[pltpu.VMEM((s, HEAD_DIM + LANES), BF16),
                        pltpu.VMEM((HEAD_DIM, s), BF16),
                        pltpu.VMEM((LANES, HEAD_DIM), F32),
                        pltpu.VMEM((HEAD_DIM + LANES, blk), BF16),
                        pltpu.VMEM((1, blk), F32),
                        pltpu.VMEM((1, blk), F32),
                        pltpu.VMEM((HEAD_DIM, blk), F32)],
        compiler_params=_params("arbitrary", "arbitrary"),
        name="attn_prompt",
    )(q, k, v)


def _attn_sample_kernel(pt_ref, q_ref, kn_ref, vn_ref, ke_ref, ko_ref, ve_ref, vo_ref, o_ref,
                        g_ref, m_ref, l_ref, acc_ref):
    del pt_ref
    j = pl.program_id(1)
    nblk = pl.num_programs(1)
    t = q_ref.shape[0]
    scale = HEAD_DIM ** -0.5
    heads = range(N_HEADS)
    q = q_ref[...]
    qh = [q[:, h * HEAD_DIM:(h + 1) * HEAD_DIM] for h in heads]

    def attend(k_heads, v_heads, mask):
        s = jnp.concatenate([_nt_dot(qh[h].astype(BF16), k_heads[h].astype(BF16)) for h in heads], axis=0)
        s = s * scale
        if mask is not None:
            s = jnp.where(mask, s, NEG)
        m = jnp.max(s, axis=1, keepdims=True)
        p = jnp.exp(s - m)
        l = jnp.sum(p, axis=1, keepdims=True)
        acc = jnp.concatenate([
            jnp.dot(p[h * t:(h + 1) * t].astype(BF16), v_heads[h].astype(BF16), preferred_element_type=F32)
            for h in heads], axis=0)
        return m, l, acc

    def paged(even_ref, odd_ref):
        return [jnp.concatenate([r[0, 0, pl.ds(h, PAGE_SIZE, stride=N_HEADS), :] for r in (even_ref, odd_ref)],
                                axis=0) for h in heads]

    k_heads = paged(ke_ref, ko_ref)
    g_ref[j] = jnp.concatenate([
        jnp.sum(qh[h] * (jnp.sum(k_heads[h], axis=0, keepdims=True) * (1.0 / MOBA_BLOCK)), axis=1, keepdims=True)
        for h in heads], axis=0)
    m, l, acc = attend(k_heads, paged(ve_ref, vo_ref), None)
    m_ref[j] = m
    l_ref[j] = l
    acc_ref[j] = acc

    @pl.when(j == nblk - 1)
    def _():
        pad = jnp.zeros((PAGE_SIZE - t, HEAD_DIM), F32)
        kn = [jnp.concatenate([kn_ref[:, h * HEAD_DIM:(h + 1) * HEAD_DIM], pad], axis=0) for h in heads]
        vn = [jnp.concatenate([vn_ref[:, h * HEAD_DIM:(h + 1) * HEAD_DIM], pad], axis=0) for h in heads]
        shape = (N_HEADS * t, PAGE_SIZE)
        tok = lax.broadcasted_iota(jnp.int32, shape, 0) % t
        key = lax.broadcasted_iota(jnp.int32, shape, 1)
        m_own, l_own, acc_own = attend(kn, vn, key <= tok)

        n = g_ref.shape[0]
        gs = [g_ref[a] for a in range(n)]
        valid = []
        for a in range(n):
            rank = jnp.zeros(gs[a].shape, jnp.int32)
            for b in range(n):
                if b == a:
                    continue
                ahead = gs[b] > gs[a]
                if b < a:
                    ahead = jnp.logical_or(ahead, gs[b] == gs[a])
                rank = rank + ahead.astype(jnp.int32)
            valid.append(rank < MOBA_TOP_K)
        m_tot = m_own
        for a in range(n):
            m_tot = jnp.maximum(m_tot, jnp.where(valid[a], m_ref[a], NEG))
        w = jnp.exp(m_own - m_tot)
        l_tot = w * l_own
        acc = w * acc_own
        for a in range(n):
            w = jnp.where(valid[a], jnp.exp(m_ref[a] - m_tot), 0.0)
            l_tot = l_tot + w * l_ref[a]
            acc = acc + w * acc_ref[a]
        out = acc / l_tot
        for h in heads:
            o_ref[:, h * HEAD_DIM:(h + 1) * HEAD_DIM] = out[h * t:(h + 1) * t, :]


def _attn_sample(layer, q, k_new, v_new, cache_k, cache_v, page_table):
    nseq, npages = page_table.shape
    t = q.shape[0] // nseq
    ppb = MOBA_BLOCK // PAGE_SIZE
    assert ppb == 2 and npages % ppb == 0 and t % SUBLANES == 0 and t <= PAGE_SIZE
    assert (npages * PAGE_SIZE) // MOBA_BLOCK == (npages * PAGE_SIZE + t - 1) // MOBA_BLOCK
    assert cache_k.shape[2:] == (PAGE_SIZE, N_HEADS, HEAD_DIM)
    nblk = npages // ppb
    pt = page_table.reshape(-1).astype(jnp.int32)
    page_rows = PAGE_SIZE * N_HEADS
    cache_k = cache_k.reshape(cache_k.shape[0], cache_k.shape[1], page_rows, HEAD_DIM)
    cache_v = cache_v.reshape(cache_v.shape[0], cache_v.shape[1], page_rows, HEAD_DIM)

    new = pl.BlockSpec((t, ATTN_W), lambda b, j, pt: (b, 0))

    def page(off):
        return pl.BlockSpec((1, 1, page_rows, HEAD_DIM),
                            lambda b, j, pt: (layer, pt[b * npages + ppb * j + off], 0, 0))

    rows = N_HEADS * t
    return pl.pallas_call(
        _attn_sample_kernel,
        grid_spec=pltpu.PrefetchScalarGridSpec(
            num_scalar_prefetch=1,
            grid=(nseq, nblk),
            in_specs=[new, new, new, page(0), page(1), page(0), page(1)],
            out_specs=new,
            scratch_shapes=[pltpu.VMEM((nblk, rows, 1), F32),
                            pltpu.VMEM((nblk, rows, 1), F32),
                            pltpu.VMEM((nblk, rows, 1), F32),
                            pltpu.VMEM((nblk, rows, HEAD_DIM), F32)]),
        out_shape=jax.ShapeDtypeStruct((nseq * t, ATTN_W), F32),
        compiler_params=_params("arbitrary", "arbitrary"),
        name="attn_sample",
    )(pt, q, k_new, v_new, cache_k, cache_k, cache_v, cache_v)


CONV_HALO = 32
CONV_CHUNK = 32


def _conv_finish(y, b_ref, g_ref, beta_ref):
    y = _layer_norm(y + b_ref[...], g_ref[...], beta_ref[...])
    return y * jax.nn.sigmoid(y)


def _conv_prompt_kernel(u_ref, halo_ref, w_ref, b_ref, g_ref, beta_ref, o_ref, buf_ref):
    i = pl.program_id(0)
    tm = u_ref.shape[0]
    buf_ref[0:CONV_HALO, :] = jnp.where(i == 0, 0.0, halo_ref[...])
    buf_ref[CONV_HALO:CONV_HALO + tm, :] = u_ref[...]
    lead = CONV_HALO - (CONV_W - 1)
    for c in range(tm // CONV_CHUNK):
        base = c * CONV_CHUNK + lead
        acc = w_ref[0:1, :] * buf_ref[base:base + CONV_CHUNK, :]
        for k in range(1, CONV_W):
            acc = acc + w_ref[k:k + 1, :] * buf_ref[base + k:base + k + CONV_CHUNK, :]
        o_ref[c * CONV_CHUNK:(c + 1) * CONV_CHUNK, :] = _conv_finish(
            acc, b_ref, g_ref, beta_ref).astype(o_ref.dtype)


def _conv_prompt(u, conv_w, conv_b, ln_g, ln_b):
    s, c = u.shape
    tm = _tile(s, 256)
    assert tm % CONV_HALO == 0 and tm % CONV_CHUNK == 0
    per = tm // CONV_HALO
    full = lambda i: (0, 0)
    return pl.pallas_call(
        _conv_prompt_kernel,
        grid=(s // tm,),
        in_specs=[pl.BlockSpec((tm, c), lambda i: (i, 0)),
                  pl.BlockSpec((CONV_HALO, c), lambda i: (jnp.maximum(i * per - 1, 0), 0)),
                  pl.BlockSpec((CONV_W, c), full),
                  pl.BlockSpec((1, c), full), pl.BlockSpec((1, c), full), pl.BlockSpec((1, c), full)],
        out_specs=pl.BlockSpec((tm, c), lambda i: (i, 0)),
        out_shape=jax.ShapeDtypeStruct((s, c), BF16),
        scratch_shapes=[pltpu.VMEM((CONV_HALO + tm, c), F32)],
        compiler_params=_params("arbitrary"),
        name="conv_prompt",
    )(u, u, conv_w, conv_b, ln_g, ln_b)


def _conv_sample_kernel(hist_ref, w_ref, b_ref, g_ref, beta_ref, o_ref):
    t = o_ref.shape[1]
    for s in range(hist_ref.shape[0]):
        acc = w_ref[0:1, :] * hist_ref[s, 0:t, :]
        for k in range(1, CONV_W):
            acc = acc + w_ref[k:k + 1, :] * hist_ref[s, k:k + t, :]
        o_ref[s] = _conv_finish(acc, b_ref, g_ref, beta_ref)


def _conv_sample(hist, conv_w, conv_b, ln_g, ln_b):
    nseq, rows, c = hist.shape
    t = rows - (CONV_W - 1)
    gs = _tile(nseq, 8)
    full = lambda i: (0, 0)
    return pl.pallas_call(
        _conv_sample_kernel,
        grid=(nseq // gs,),
        in_specs=[pl.BlockSpec((gs, rows, c), lambda i: (i, 0, 0)),
                  pl.BlockSpec((CONV_W, c), full),
                  pl.BlockSpec((1, c), full), pl.BlockSpec((1, c), full), pl.BlockSpec((1, c), full)],
        out_specs=pl.BlockSpec((gs, t, c), lambda i: (i, 0, 0)),
        out_shape=jax.ShapeDtypeStruct((nseq, t, c), F32),
        compiler_params=_params("arbitrary"),
        name="conv_sample",
    )(hist, conv_w, conv_b, ln_g, ln_b)


def _outproj_kernel(alpha, attn_ref, conv_ref, x_ref, wa_ref, wc_ref, g_ref, b_ref, h_ref, hb_ref):
    mixed = jnp.dot(attn_ref[...].astype(BF16), wa_ref[...], preferred_element_type=F32)
    mixed = mixed + jnp.dot(conv_ref[...].astype(BF16), wc_ref[...], preferred_element_type=F32)
    h = _layer_norm(alpha * x_ref[...] + mixed, g_ref[...], b_ref[...])
    h_ref[...] = h
    hb_ref[...] = h.astype(BF16)


def _outproj(alpha, attn, conv, x, w_out_b, ln_g, ln_b):
    n, d = x.shape
    wa = attn.shape[1]
    wc = conv.shape[1]
    tm = _tile(n, 256)
    row = lambda i: (i, 0)
    full = lambda i: (0, 0)
    return pl.pallas_call(
        functools.partial(_outproj_kernel, alpha),
        grid=(n // tm,),
        in_specs=[pl.BlockSpec((tm, wa), row), pl.BlockSpec((tm, wc), row), pl.BlockSpec((tm, d), row),
                  pl.BlockSpec((wa, d), full), pl.BlockSpec((wc, d), lambda i: (wa // wc, 0)),
                  pl.BlockSpec((1, d), full), pl.BlockSpec((1, d), full)],
        out_specs=[pl.BlockSpec((tm, d), row)] * 2,
        out_shape=[jax.ShapeDtypeStruct((n, d), F32), jax.ShapeDtypeStruct((n, d), BF16)],
        compiler_params=_params("arbitrary"),
        name="outproj",
    )(attn, conv, x, w_out_b, w_out_b, ln_g, ln_b)


def _top_sorted(x, k):
    out = []
    for _ in range(k):
        mx = jnp.max(x, axis=0, keepdims=True)
        out.append(mx)
        x = jnp.where(x >= mx, -jnp.inf, x)
    return out


def _route_kernel(hb_ref, wq_ref, keys_ref, c_ref, bs_ref, wa_ref, eb_ref):
    q = jnp.dot(hb_ref[...], wq_ref[...], preferred_element_type=F32)
    k = PEER_TOPK
    for h in range(PEER_HEADS):
        lo = h * 2 * PEER_HALF
        sa = _nt_dot(keys_ref[h, 0], q[:, lo:lo + PEER_HALF], precision=lax.Precision.HIGHEST)
        sb = _nt_dot(keys_ref[h, 1], q[:, lo + PEER_HALF:lo + 2 * PEER_HALF],
                     precision=lax.Precision.HIGHEST)
        ta = _top_sorted(sa, k + 1)
        tb_rows = _top_sorted(sb, k + 1)
        tb = jnp.concatenate(tb_rows[:k], axis=0)
        half = k // 2
        cand = [ta[0] + tb]
        cand += [ta[i] + tb[0:half] for i in range(1, half)]
        cand += [jnp.concatenate(ta[half:k], axis=0) + tb[0:1]]
        row = lax.broadcasted_iota(jnp.int32, (SUBLANES, sa.shape[1]), 0)
        cand += [jnp.where(row == 0, ta[k] + tb_rows[0],
                           jnp.where(row == 1, ta[0] + tb_rows[k], -jnp.inf))]
        cand = jnp.concatenate(cand, axis=0)
        best = _top_sorted(cand, k + 1)
        tau = 0.5 * (best[k - 1] + best[k])
        top = ta[0] + tb[0:1]
        z = jnp.sum(jnp.where(cand >= tau, jnp.exp(cand - top), 0.0), axis=0, keepdims=True)
        c_ref[h] = tau - sa
        bs_ref[h] = sb
        wa_ref[h] = jnp.exp(sa - ta[0]) / z
        eb_ref[h] = jnp.exp(sb - tb[0:1])


def _route(hb, wq_b, keys):
    n, d = hb.shape
    tm = _tile(n, 256)
    out = jax.ShapeDtypeStruct((PEER_HEADS, PEER_KEYS, n), F32)
    ospec = pl.BlockSpec((PEER_HEADS, PEER_KEYS, tm), lambda i: (0, 0, i))
    return pl.pallas_call(
        _route_kernel,
        grid=(n // tm,),
        in_specs=[pl.BlockSpec((tm, d), lambda i: (i, 0)),
                  pl.BlockSpec(wq_b.shape, lambda i: (0, 0)),
                  pl.BlockSpec(keys.shape, lambda i: (0, 0, 0, 0))],
        out_specs=[ospec] * 4,
        out_shape=[out] * 4,
        compiler_params=_params("arbitrary"),
        name="route",
    )(hb, wq_b, keys)


def _expert_kernel(hb_ref, u_ref, vt_ref, c_ref, bs_ref, wa_ref, eb_ref, o_ref, acc_ref, xu_ref, wgt_ref):
    e = pl.program_id(1)
    nblocks = pl.num_programs(1) - 2
    groups = u_ref.shape[0] // PEER_KEYS
    cur = e % 2
    prev = 1 - cur

    @pl.when(e == 0)
    def _():
        acc_ref[...] = jnp.zeros_like(acc_ref)
        xu_ref[1] = jnp.zeros(xu_ref.shape[1:], F32)
        wgt_ref[1] = jnp.zeros(wgt_ref.shape[1:], BF16)

    acc_ref[...] += jnp.dot(vt_ref[...], wgt_ref[prev], preferred_element_type=F32)

    block = jnp.clip(e - 1, 0, nblocks - 1)
    xu = xu_ref[prev]
    act = 0.5 * xu * (1.0 + lax.erf(xu * (1.0 / math.sqrt(2.0))))
    for r in range(groups):
        a = block * groups + r
        g = None
        for h in range(PEER_HEADS):
            c = c_ref[h, pl.ds(a, 1), :]
            w = wa_ref[h, pl.ds(a, 1), :]
            term = jnp.where(bs_ref[h] >= c, eb_ref[h], 0.0) * w
            g = term if g is None else g + term
        rows = slice(r * PEER_KEYS, (r + 1) * PEER_KEYS)
        wgt_ref[cur, rows, :] = (g * act[rows]).astype(BF16)

    xu_ref[cur] = _nt_dot(u_ref[...], hb_ref[...])

    @pl.when(e == pl.num_programs(1) - 1)
    def _():
        o_ref[...] = acc_ref[...].T


EXPERT_BLOCK = 512


def _expert_blocks_t(v):
    ne, d = v.shape
    te = _tile(ne, EXPERT_BLOCK)
    return v.reshape(ne // te, te, d).transpose(0, 2, 1)


def _expert(hb, u_b, vt_b, c, bs, wa, eb):
    n, d = hb.shape
    nblocks, _, te = vt_b.shape
    tm = _tile(n, 1024)
    once = pl.Buffered(1)
    rspec = pl.BlockSpec((PEER_HEADS, PEER_KEYS, tm), lambda i, e: (0, 0, i), pipeline_mode=once)
    return pl.pallas_call(
        _expert_kernel,
        grid=(n // tm, nblocks + 2),
        in_specs=[pl.BlockSpec((tm, d), lambda i, e: (i, 0), pipeline_mode=once),
                  pl.BlockSpec((te, d), lambda i, e: (jnp.minimum(e, nblocks - 1), 0)),
                  pl.BlockSpec((None, d, te), lambda i, e: (jnp.maximum(e - 2, 0), 0, 0)),
                  rspec, rspec, rspec, rspec],
        out_specs=pl.BlockSpec((tm, d), lambda i, e: (i, 0), pipeline_mode=once),
        out_shape=jax.ShapeDtypeStruct((n, d), F32),
        scratch_shapes=[pltpu.VMEM((d, tm), F32), pltpu.VMEM((2, te, tm), F32), pltpu.VMEM((2, te, tm), BF16)],
        compiler_params=_params("arbitrary", "arbitrary"),
        name="expert",
    )(hb, u_b, vt_b, c, bs, wa, eb)


def _final_kernel(alpha, h_ref, peer_ref, p_ref, gw_ref, gb_ref, pw_ref, g_ref, b_ref, y_ref):
    h2 = _layer_norm(alpha * h_ref[...] + peer_ref[...], g_ref[...], b_ref[...])
    gate = jax.nn.sigmoid(jnp.dot(h2.astype(BF16), gw_ref[...], preferred_element_type=F32) + gb_ref[...])
    emb = jnp.dot(p_ref[...].astype(BF16), pw_ref[...], preferred_element_type=F32)
    y_ref[...] = h2 + gate * emb


def _final(alpha, h, peer_out, p, gate_w_b, gate_b, ple_w_b, ln_g, ln_b):
    n, d = h.shape
    pd = p.shape[1]
    tm = _tile(n, 256)
    row = lambda i: (i, 0)
    full = lambda i: (0, 0)
    return pl.pallas_call(
        functools.partial(_final_kernel, alpha),
        grid=(n // tm,),
        in_specs=[pl.BlockSpec((tm, d), row), pl.BlockSpec((tm, d), row), pl.BlockSpec((tm, pd), row),
                  pl.BlockSpec((d, d), full), pl.BlockSpec((1, d), full), pl.BlockSpec((pd, d), full),
                  pl.BlockSpec((1, d), full), pl.BlockSpec((1, d), full)],
        out_specs=pl.BlockSpec((tm, d), row),
        out_shape=jax.ShapeDtypeStruct((n, d), F32),
        compiler_params=_params("arbitrary"),
        name="final",
    )(h, peer_out, p, gate_w_b, gate_b, ple_w_b, ln_g, ln_b)


def _channel_mix(alpha, x, p, attn, conv, lw):
    h, hb = _outproj(alpha, attn, conv, x, lw["w_out"], lw["ln1_g"], lw["ln1_b"])
    c, bs, wa, eb = _route(hb, lw["peer_wq"], lw["peer_keys"])
    peer_out = _expert(hb, lw["peer_u"], lw["peer_vt"], c, bs, wa, eb)
    return _final(alpha, h, peer_out, p, lw["gate_w"], lw["gate_b"], lw["ple_w"], lw["ln2_g"], lw["ln2_b"])


def kernel(x_prompt, x_sample, cache_k, cache_v, state_conv, page_table, p_prompt, p_sample, w_in, w_out, conv_w, conv_b, conv_ln_g, conv_ln_b, ln1_g, ln1_b, peer_wq, peer_keys, peer_u, peer_v, ln2_g, ln2_b, ple_w, gate_w, gate_b):
    depth = w_in.shape[0]
    batch, seq, d = x_prompt.shape
    nseq, dec_seq, _ = x_sample.shape
    c_conv = conv_w.shape[2]
    past_len = page_table.shape[1] * PAGE_SIZE
    alpha = (2.0 * depth) ** 0.25
    assert batch == 1 and w_in.shape[2] == 3 * ATTN_W + 2 * c_conv and c_conv == ATTN_W
    assert peer_keys.shape[1:] == (PEER_HEADS, 2, PEER_KEYS, PEER_HALF)

    cos_p, sin_p = _rotary_tables(jnp.arange(seq, dtype=jnp.int32))
    cos_s, sin_s = _rotary_tables(past_len + jnp.arange(dec_seq, dtype=jnp.int32))
    cos_s = jnp.tile(cos_s, (nseq, 1))
    sin_s = jnp.tile(sin_s, (nseq, 1))

    xp = x_prompt.reshape(seq, d)
    xs = x_sample.reshape(nseq * dec_seq, d)
    outs = [[] for _ in range(6)]
    for i in range(depth):
        vec = lambda a: a[i].reshape(1, -1)
        lw = dict(w_out=w_out[i].astype(BF16), ln1_g=vec(ln1_g), ln1_b=vec(ln1_b),
                  peer_wq=peer_wq[i].astype(BF16), peer_keys=peer_keys[i],
                  peer_u=peer_u[i].astype(BF16), peer_vt=_expert_blocks_t(peer_v[i].astype(BF16)),
                  gate_w=gate_w[i].astype(BF16), gate_b=vec(gate_b), ple_w=ple_w[i].astype(BF16),
                  ln2_g=vec(ln2_g), ln2_b=vec(ln2_b))
        cw = (conv_w[i], vec(conv_b), vec(conv_ln_g), vec(conv_ln_b))
        w_in_b = w_in[i].astype(BF16)

        qp, kp, vp, up = _proj(xp, w_in_b, cos_p, sin_p)
        attn_p = _attn_prompt(qp, kp, vp)
        conv_p = _conv_prompt(up, *cw)
        xp_next = _channel_mix(alpha, xp, p_prompt[i].reshape(seq, -1), attn_p, conv_p, lw)

        qs, ks, vs, us = _proj(xs, w_in_b, cos_s, sin_s)
        attn_s = _attn_sample(i, qs, ks, vs, cache_k, cache_v, page_table)
        hist_s = jnp.concatenate([state_conv[i], us.reshape(nseq, dec_seq, c_conv)], axis=1)
        conv_s = _conv_sample(hist_s, *cw).reshape(nseq * dec_seq, c_conv)
        xs = _channel_mix(alpha, xs, p_sample[i].reshape(nseq * dec_seq, -1), attn_s, conv_s, lw)
        xp = xp_next

        hist_tail = jnp.concatenate([jnp.zeros((CONV_W - 1, c_conv), F32), up], axis=0)[-(CONV_W - 1):]
        outs[0].append(kp.reshape(batch, seq, N_HEADS, HEAD_DIM))
        outs[1].append(vp.reshape(batch, seq, N_HEADS, HEAD_DIM))
        outs[2].append(hist_tail.reshape(batch, CONV_W - 1, c_conv))
        outs[3].append(ks.reshape(nseq, dec_seq, N_HEADS, HEAD_DIM))
        outs[4].append(vs.reshape(nseq, dec_seq, N_HEADS, HEAD_DIM))
        outs[5].append(hist_s[:, -(CONV_W - 1):])
    kp_l, vp_l, cp_l, ks_l, vs_l, cs_l = [jnp.stack(o) for o in outs]
    return (xp.reshape(batch, seq, d), xs.reshape(nseq, dec_seq, d), kp_l, vp_l, cp_l, ks_l, vs_l, cs_l)
```

```python
import functools
import math

import jax
import jax.numpy as jnp
from jax import lax
from jax.experimental import pallas as pl
from jax.experimental.pallas import tpu as pltpu

N_HEADS = 8
HEAD_DIM = 128
ATTN_W = N_HEADS * HEAD_DIM
CONV_W = 31
MOBA_BLOCK = 256
MOBA_TOP_K = 3
ROPE_THETA = 10000.0
PAGE_SIZE = 128
PEER_HEADS = 8
PEER_KEYS = 128
PEER_TOPK = 16
PEER_HALF = 128
LN_EPS = 1e-5

LANES = 128
SUBLANES = 8
MXU_DEPTH = 256
VMEM_LIMIT = 56 * 1024 * 1024

NEG = -1e30
F32 = jnp.float32
BF16 = jnp.bfloat16
NT_DIMS = (((1,), (1,)), ((), ()))


def _nt_dot(a, b, precision=None):
    return lax.dot_general(a, b, NT_DIMS, precision=precision, preferred_element_type=F32)


def _layer_norm(x, g, b):
    mu = jnp.mean(x, axis=-1, keepdims=True)
    xc = x - mu
    var = jnp.mean(xc * xc, axis=-1, keepdims=True)
    return xc * lax.rsqrt(var + LN_EPS) * g + b


def _params(*sem):
    return pltpu.CompilerParams(dimension_semantics=sem, vmem_limit_bytes=VMEM_LIMIT)


def _tile(n, t):
    t = min(n, t)
    assert n % t == 0, (n, t)
    return t


def _proj_kernel(x_ref, w_ref, cos_ref, sin_ref, q_ref, k_ref, v_ref, u_ref, xb_ref, a_ref):
    j = pl.program_id(1)

    @pl.when(j == 0)
    def _():
        xb_ref[...] = x_ref[...].astype(BF16)

    z = jnp.dot(xb_ref[...], w_ref[...], preferred_element_type=F32)

    def rotary_to(o_ref):
        c = cos_ref[...]
        s = sin_ref[...]
        for h in range(N_HEADS):
            zh = z[:, h * HEAD_DIM:(h + 1) * HEAD_DIM]
            o_ref[:, h * HEAD_DIM:(h + 1) * HEAD_DIM] = zh * c + pltpu.roll(zh, HEAD_DIM // 2, 1) * s

    @pl.when(j == 0)
    def _():
        rotary_to(q_ref)

    @pl.when(j == 1)
    def _():
        rotary_to(k_ref)

    @pl.when(j == 2)
    def _():
        v_ref[...] = z

    @pl.when(j == 3)
    def _():
        a_ref[...] = z

    @pl.when(j == 4)
    def _():
        u_ref[...] = a_ref[...] * jax.nn.sigmoid(z)


def _proj(x, w_in_b, cos_t, sin_t):
    n, d = x.shape
    tm = _tile(n, 512)
    row = lambda i, j: (i, 0)
    out = jax.ShapeDtypeStruct((n, ATTN_W), F32)
    return pl.pallas_call(
        _proj_kernel,
        grid=(n // tm, 5),
        in_specs=[pl.BlockSpec((tm, d), row),
                  pl.BlockSpec((d, ATTN_W), lambda i, j: (0, j)),
                  pl.BlockSpec((tm, HEAD_DIM), row),
                  pl.BlockSpec((tm, HEAD_DIM), row)],
        out_specs=[pl.BlockSpec((tm, ATTN_W), row)] * 4,
        out_shape=[out] * 4,
        scratch_shapes=[pltpu.VMEM((tm, d), BF16), pltpu.VMEM((tm, ATTN_W), F32)],
        compiler_params=_params("arbitrary", "arbitrary"),
        name="proj",
    )(x, w_in_b, cos_t, sin_t)


def _rotary_tables(pos):
    half = HEAD_DIM // 2
    inv = ROPE_THETA ** (-jnp.arange(half, dtype=F32) * 2.0 / HEAD_DIM)
    ang = pos.astype(F32)[:, None] * inv[None, :]
    c, s = jnp.cos(ang), jnp.sin(ang)
    return jnp.concatenate([c, c], axis=1), jnp.concatenate([-s, s], axis=1)


EXP2_SCALE = HEAD_DIM ** -0.5 * math.log2(math.e)


def _attn_prompt_kernel(grp, q_ref, k_ref, v_ref, o_ref,
                        kaug_ref, vt_ref, kmean_ref, qaug_ref, m_ref, l_ref, acc_ref):
    i = pl.program_id(1)
    blk = MOBA_BLOCK
    nb = k_ref.shape[0] // blk
    nbp = -(-nb // SUBLANES) * SUBLANES
    heads = range(kaug_ref.shape[0])
    cols = [slice(h * HEAD_DIM, (h + 1) * HEAD_DIM) for h in heads]

    @pl.when(i == 0)
    def _():
        kmean_ref[...] = jnp.zeros_like(kmean_ref)

        def fill(j, carry):
            r = pl.multiple_of(j * blk, blk)
            hot = (lax.broadcasted_iota(jnp.int32, (blk, LANES), 1) == j).astype(F32).astype(BF16)
            for h in heads:
                kb = k_ref[pl.ds(r, blk), cols[h]]
                kmean_ref[h, pl.ds(j, 1), :] = jnp.sum(kb, axis=0, keepdims=True) * (1.0 / blk)
                kaug_ref[h, pl.ds(r, blk), 0:HEAD_DIM] = kb.astype(BF16)
                kaug_ref[h, pl.ds(r, blk), HEAD_DIM:HEAD_DIM + LANES] = hot
                vt_ref[h, :, pl.ds(r, blk)] = v_ref[pl.ds(r, blk), cols[h]].T.astype(BF16)
            return carry

        lax.fori_loop(0, nb, fill, 0)

    for h in heads:
        q = q_ref[:, cols[h]]
        gate = _nt_dot(kmean_ref[h, 0:nbp, :], q, precision=lax.Precision.HIGHEST)
        row = lax.broadcasted_iota(jnp.int32, gate.shape, 0)
        past = row < i
        work = jnp.where(past, gate, -jnp.inf)
        sel = jnp.zeros(gate.shape, jnp.bool_)
        for _ in range(MOBA_TOP_K):
            mx = jnp.max(work, axis=0, keepdims=True)
            idx = jnp.min(jnp.where(work == mx, row, jnp.int32(1 << 30)), axis=0, keepdims=True)
            pick = row == idx
            sel = jnp.logical_or(sel, pick)
            work = jnp.where(pick, -jnp.inf, work)
        keep = jnp.logical_or(jnp.logical_and(sel, past), row == i)
        pen = jnp.where(keep, 0.0, NEG)
        pen = jnp.concatenate([pen, jnp.zeros((LANES - nbp, blk), F32)], axis=0)
        qaug_ref[h, 0:HEAD_DIM, :] = q.T.astype(BF16)
        qaug_ref[h, HEAD_DIM:HEAD_DIM + LANES, :] = pen.astype(BF16)

    gk = grp * blk
    gi = i // grp

    def scores(g):
        r = pl.multiple_of(g * gk, gk)
        return [jnp.dot(kaug_ref[h, pl.ds(r, gk), :], qaug_ref[h], preferred_element_type=F32)
                for h in heads], r

    ss, r = scores(gi)
    key_pos = lax.broadcasted_iota(jnp.int32, ss[0].shape, 0) + gi * gk
    qry_pos = lax.broadcasted_iota(jnp.int32, ss[0].shape, 1) + i * blk
    causal = key_pos <= qry_pos
    for h in heads:
        s = jnp.where(causal, ss[h], NEG)
        m0 = jnp.max(s, axis=0, keepdims=True)
        p = jnp.exp2((s - m0) * EXP2_SCALE)
        m_ref[h] = m0
        l_ref[h] = jnp.sum(p, axis=0, keepdims=True)
        acc_ref[h] = jnp.dot(vt_ref[h, :, pl.ds(r, gk)], p.astype(BF16), preferred_element_type=F32)

    def body(g, carry):
        ss, r = scores(g)
        for h in heads:
            s = ss[h]
            m_old = m_ref[h]
            m_new = jnp.maximum(m_old, jnp.max(s, axis=0, keepdims=True))
            alpha = jnp.exp2((m_old - m_new) * EXP2_SCALE)
            p = jnp.exp2((s - m_new) * EXP2_SCALE)
            l_ref[h] = alpha * l_ref[h] + jnp.sum(p, axis=0, keepdims=True)
            acc_ref[h] = alpha * acc_ref[h] + jnp.dot(
                vt_ref[h, :, pl.ds(r, gk)], p.astype(BF16), preferred_element_type=F32)
            m_ref[h] = m_new
        return carry

    lax.fori_loop(0, gi, body, 0)
    for h in heads:
        o_ref[:, cols[h]] = (acc_ref[h] / l_ref[h]).T.astype(o_ref.dtype)


ATTN_HEADS_PER_STEP = 2


def _attn_prompt(q, k, v):
    s = q.shape[0]
    blk = MOBA_BLOCK
    nb = s // blk
    assert s % blk == 0 and nb <= LANES
    grp = next(g for g in (4, 2, 1) if nb % g == 0)
    hp = ATTN_HEADS_PER_STEP
    width = hp * HEAD_DIM
    qspec = pl.BlockSpec((blk, width), lambda h, i: (i, h))
    kvspec = pl.BlockSpec((s, width), lambda h, i: (0, h), pipeline_mode=pl.Buffered(1))
    return pl.pallas_call(
        functools.partial(_attn_prompt_kernel, grp),
        grid=(N_HEADS // hp, nb),
        in_specs=[qspec, kvspec, kvspec],
        out_specs=qspec,
        out_shape=jax.ShapeDtypeStruct((s, ATTN_W), BF16),
        scratch_shapes=[pltpu.VMEM((hp, s, HEAD_DIM + LANES), BF16),
                        pltpu.VMEM((hp, HEAD_DIM, s), BF16),
                        pltpu.VMEM((hp, LANES, HEAD_DIM), F32),
                        pltpu.VMEM((hp, HEAD_DIM + LANES, blk), BF16),
                        pltpu.VMEM((hp, 1, blk), F32),
                        pltpu.VMEM((hp, 1, blk), F32),
                        pltpu.VMEM((hp, HEAD_DIM, blk), F32)],
        compiler_params=_params("arbitrary", "arbitrary"),
        name="attn_prompt",
    )(q, k, v)


def _attn_sample_kernel(pt_ref, q_ref, kn_ref, vn_ref, ke_ref, ko_ref, ve_ref, vo_ref, o_ref,
                        g_ref, m_ref, l_ref, acc_ref):
    del pt_ref
    j = pl.program_id(1)
    nblk = pl.num_programs(1)
    t = q_ref.shape[0]
    scale = HEAD_DIM ** -0.5
    heads = range(N_HEADS)
    q = q_ref[...]
    qh = [q[:, h * HEAD_DIM:(h + 1) * HEAD_DIM] for h in heads]

    def attend(k_heads, v_heads, mask):
        s = jnp.concatenate([_nt_dot(qh[h].astype(BF16), k_heads[h].astype(BF16)) for h in heads], axis=0)
        s = s * scale
        if mask is not None:
            s = jnp.where(mask, s, NEG)
        m = jnp.max(s, axis=1, keepdims=True)
        p = jnp.exp(s - m)
        l = jnp.sum(p, axis=1, keepdims=True)
        acc = jnp.concatenate([
            jnp.dot(p[h * t:(h + 1) * t].astype(BF16), v_heads[h].astype(BF16), preferred_element_type=F32)
            for h in heads], axis=0)
        return m, l, acc

    def paged(even_ref, odd_ref):
        return [jnp.concatenate([r[0, 0, pl.ds(h, PAGE_SIZE, stride=N_HEADS), :] for r in (even_ref, odd_ref)],
                                axis=0) for h in heads]

    k_heads = paged(ke_ref, ko_ref)
    g_ref[j] = jnp.concatenate([
        jnp.sum(qh[h] * (jnp.sum(k_heads[h], axis=0, keepdims=True) * (1.0 / MOBA_BLOCK)), axis=1, keepdims=True)
        for h in heads], axis=0)
    m, l, acc = attend(k_heads, paged(ve_ref, vo_ref), None)
    m_ref[j] = m
    l_ref[j] = l
    acc_ref[j] = acc

    @pl.when(j == nblk - 1)
    def _():
        pad = jnp.zeros((PAGE_SIZE - t, HEAD_DIM), F32)
        kn = [jnp.concatenate([kn_ref[:, h * HEAD_DIM:(h + 1) * HEAD_DIM], pad], axis=0) for h in heads]
        vn = [jnp.concatenate([vn_ref[:, h * HEAD_DIM:(h + 1) * HEAD_DIM], pad], axis=0) for h in heads]
        shape = (N_HEADS * t, PAGE_SIZE)
        tok = lax.broadcasted_iota(jnp.int32, shape, 0) % t
        key = lax.broadcasted_iota(jnp.int32, shape, 1)
        m_own, l_own, acc_own = attend(kn, vn, key <= tok)

        n = g_ref.shape[0]
        gs = [g_ref[a] for a in range(n)]
        valid = []
        for a in range(n):
            rank = jnp.zeros(gs[a].shape, jnp.int32)
            for b in range(n):
                if b == a:
                    continue
                ahead = gs[b] > gs[a]
                if b < a:
                    ahead = jnp.logical_or(ahead, gs[b] == gs[a])
                rank = rank + ahead.astype(jnp.int32)
            valid.append(rank < MOBA_TOP_K)
        m_tot = m_own
        for a in range(n):
            m_tot = jnp.maximum(m_tot, jnp.where(valid[a], m_ref[a], NEG))
        w = jnp.exp(m_own - m_tot)
        l_tot = w * l_own
        acc = w * acc_own
        for a in range(n):
            w = jnp.where(valid[a], jnp.exp(m_ref[a] - m_tot), 0.0)
            l_tot = l_tot + w * l_ref[a]
            acc = acc + w * acc_ref[a]
        out = acc / l_tot
        for h in heads:
            o_ref[:, h * HEAD_DIM:(h + 1) * HEAD_DIM] = out[h * t:(h + 1) * t, :]


def _attn_sample(layer, q, k_new, v_new, cache_k, cache_v, page_table):
    nseq, npages = page_table.shape
    t = q.shape[0] // nseq
    ppb = MOBA_BLOCK // PAGE_SIZE
    assert ppb == 2 and npages % ppb == 0 and t % SUBLANES == 0 and t <= PAGE_SIZE
    assert (npages * PAGE_SIZE) // MOBA_BLOCK == (npages * PAGE_SIZE + t - 1) // MOBA_BLOCK
    assert cache_k.shape[2:] == (PAGE_SIZE, N_HEADS, HEAD_DIM)
    nblk = npages // ppb
    pt = page_table.reshape(-1).astype(jnp.int32)
    page_rows = PAGE_SIZE * N_HEADS
    cache_k = cache_k.reshape(cache_k.shape[0], cache_k.shape[1], page_rows, HEAD_DIM)
    cache_v = cache_v.reshape(cache_v.shape[0], cache_v.shape[1], page_rows, HEAD_DIM)

    new = pl.BlockSpec((t, ATTN_W), lambda b, j, pt: (b, 0))

    def page(off):
        return pl.BlockSpec((1, 1, page_rows, HEAD_DIM),
                            lambda b, j, pt: (layer, pt[b * npages + ppb * j + off], 0, 0))

    rows = N_HEADS * t
    return pl.pallas_call(
        _attn_sample_kernel,
        grid_spec=pltpu.PrefetchScalarGridSpec(
            num_scalar_prefetch=1,
            grid=(nseq, nblk),
            in_specs=[new, new, new, page(0), page(1), page(0), page(1)],
            out_specs=new,
            scratch_shapes=[pltpu.VMEM((nblk, rows, 1), F32),
                            pltpu.VMEM((nblk, rows, 1), F32),
                            pltpu.VMEM((nblk, rows, 1), F32),
                            pltpu.VMEM((nblk, rows, HEAD_DIM), F32)]),
        out_shape=jax.ShapeDtypeStruct((nseq * t, ATTN_W), F32),
        compiler_params=_params("arbitrary", "arbitrary"),
        name="attn_sample",
    )(pt, q, k_new, v_new, cache_k, cache_k, cache_v, cache_v)


CONV_HALO = 32
CONV_CHUNK = 32


def _conv_finish(y, b_ref, g_ref, beta_ref):
    y = _layer_norm(y + b_ref[...], g_ref[...], beta_ref[...])
    return y * jax.nn.sigmoid(y)


def _conv_prompt_kernel(u_ref, halo_ref, w_ref, b_ref, g_ref, beta_ref, o_ref, buf_ref):
    i = pl.program_id(0)
    tm = u_ref.shape[0]
    buf_ref[0:CONV_HALO, :] = jnp.where(i == 0, 0.0, halo_ref[...])
    buf_ref[CONV_HALO:CONV_HALO + tm, :] = u_ref[...]
    lead = CONV_HALO - (CONV_W - 1)
    for c in range(tm // CONV_CHUNK):
        base = c * CONV_CHUNK + lead
        acc = w_ref[0:1, :] * buf_ref[base:base + CONV_CHUNK, :]
        for k in range(1, CONV_W):
            acc = acc + w_ref[k:k + 1, :] * buf_ref[base + k:base + k + CONV_CHUNK, :]
        o_ref[c * CONV_CHUNK:(c + 1) * CONV_CHUNK, :] = _conv_finish(
            acc, b_ref, g_ref, beta_ref).astype(o_ref.dtype)


def _conv_prompt(u, conv_w, conv_b, ln_g, ln_b):
    s, c = u.shape
    tm = _tile(s, 256)
    assert tm % CONV_HALO == 0 and tm % CONV_CHUNK == 0
    per = tm // CONV_HALO
    full = lambda i: (0, 0)
    return pl.pallas_call(
        _conv_prompt_kernel,
        grid=(s // tm,),
        in_specs=[pl.BlockSpec((tm, c), lambda i: (i, 0)),
                  pl.BlockSpec((CONV_HALO, c), lambda i: (jnp.maximum(i * per - 1, 0), 0)),
                  pl.BlockSpec((CONV_W, c), full),
                  pl.BlockSpec((1, c), full), pl.BlockSpec((1, c), full), pl.BlockSpec((1, c), full)],
        out_specs=pl.BlockSpec((tm, c), lambda i: (i, 0)),
        out_shape=jax.ShapeDtypeStruct((s, c), BF16),
        scratch_shapes=[pltpu.VMEM((CONV_HALO + tm, c), F32)],
        compiler_params=_params("arbitrary"),
        name="conv_prompt",
    )(u, u, conv_w, conv_b, ln_g, ln_b)


def _conv_sample_kernel(hist_ref, w_ref, b_ref, g_ref, beta_ref, o_ref):
    t = o_ref.shape[1]
    for s in range(hist_ref.shape[0]):
        acc = w_ref[0:1, :] * hist_ref[s, 0:t, :]
        for k in range(1, CONV_W):
            acc = acc + w_ref[k:k + 1, :] * hist_ref[s, k:k + t, :]
        o_ref[s] = _conv_finish(acc, b_ref, g_ref, beta_ref)


def _conv_sample(hist, conv_w, conv_b, ln_g, ln_b):
    nseq, rows, c = hist.shape
    t = rows - (CONV_W - 1)
    gs = _tile(nseq, 8)
    full = lambda i: (0, 0)
    return pl.pallas_call(
        _conv_sample_kernel,
        grid=(nseq // gs,),
        in_specs=[pl.BlockSpec((gs, rows, c), lambda i: (i, 0, 0)),
                  pl.BlockSpec((CONV_W, c), full),
                  pl.BlockSpec((1, c), full), pl.BlockSpec((1, c), full), pl.BlockSpec((1, c), full)],
        out_specs=pl.BlockSpec((gs, t, c), lambda i: (i, 0, 0)),
        out_shape=jax.ShapeDtypeStruct((nseq, t, c), F32),
        compiler_params=_params("arbitrary"),
        name="conv_sample",
    )(hist, conv_w, conv_b, ln_g, ln_b)


def _outproj_kernel(alpha, attn_ref, conv_ref, x_ref, wa_ref, wc_ref, g_ref, b_ref, h_ref, hb_ref):
    mixed = jnp.dot(attn_ref[...].astype(BF16), wa_ref[...], preferred_element_type=F32)
    mixed = mixed + jnp.dot(conv_ref[...].astype(BF16), wc_ref[...], preferred_element_type=F32)
    h = _layer_norm(alpha * x_ref[...] + mixed, g_ref[...], b_ref[...])
    h_ref[...] = h
    hb_ref[...] = h.astype(BF16)


def _outproj(alpha, attn, conv, x, w_out_b, ln_g, ln_b):
    n, d = x.shape
    wa = attn.shape[1]
    wc = conv.shape[1]
    tm = _tile(n, 256)
    row = lambda i: (i, 0)
    full = lambda i: (0, 0)
    return pl.pallas_call(
        functools.partial(_outproj_kernel, alpha),
        grid=(n // tm,),
        in_specs=[pl.BlockSpec((tm, wa), row), pl.BlockSpec((tm, wc), row), pl.BlockSpec((tm, d), row),
                  pl.BlockSpec((wa, d), full), pl.BlockSpec((wc, d), lambda i: (wa // wc, 0)),
                  pl.BlockSpec((1, d), full), pl.BlockSpec((1, d), full)],
        out_specs=[pl.BlockSpec((tm, d), row)] * 2,
        out_shape=[jax.ShapeDtypeStruct((n, d), F32), jax.ShapeDtypeStruct((n, d), BF16)],
        compiler_params=_params("arbitrary"),
        name="outproj",
    )(attn, conv, x, w_out_b, w_out_b, ln_g, ln_b)


def _top_sorted(x, k):
    out = []
    for _ in range(k):
        mx = jnp.max(x, axis=0, keepdims=True)
        out.append(mx)
        x = jnp.where(x >= mx, -jnp.inf, x)
    return out


def _route_kernel(hb_ref, wq_ref, keys_ref, c_ref, bs_ref, wa_ref, eb_ref):
    q = jnp.dot(hb_ref[...], wq_ref[...], preferred_element_type=F32)
    k = PEER_TOPK
    for h in range(PEER_HEADS):
        lo = h * 2 * PEER_HALF
        sa = _nt_dot(keys_ref[h, 0], q[:, lo:lo + PEER_HALF], precision=lax.Precision.HIGHEST)
        sb = _nt_dot(keys_ref[h, 1], q[:, lo + PEER_HALF:lo + 2 * PEER_HALF],
                     precision=lax.Precision.HIGHEST)
        ta = _top_sorted(sa, k + 1)
        tb_rows = _top_sorted(sb, k + 1)
        tb = jnp.concatenate(tb_rows[:k], axis=0)
        half = k // 2
        cand = [ta[0] + tb]
        cand += [ta[i] + tb[0:half] for i in range(1, half)]
        cand += [jnp.concatenate(ta[half:k], axis=0) + tb[0:1]]
        row = lax.broadcasted_iota(jnp.int32, (SUBLANES, sa.shape[1]), 0)
        cand += [jnp.where(row == 0, ta[k] + tb_rows[0],
                           jnp.where(row == 1, ta[0] + tb_rows[k], -jnp.inf))]
        cand = jnp.concatenate(cand, axis=0)
        best = _top_sorted(cand, k + 1)
        tau = 0.5 * (best[k - 1] + best[k])
        top = ta[0] + tb[0:1]
        z = jnp.sum(jnp.where(cand >= tau, jnp.exp(cand - top), 0.0), axis=0, keepdims=True)
        c_ref[h] = tau - sa
        bs_ref[h] = sb
        wa_ref[h] = jnp.exp(sa - ta[0]) / z
        eb_ref[h] = jnp.exp(sb - tb[0:1])


def _route(hb, wq_b, keys):
    n, d = hb.shape
    tm = _tile(n, 256)
    out = jax.ShapeDtypeStruct((PEER_HEADS, PEER_KEYS, n), F32)
    ospec = pl.BlockSpec((PEER_HEADS, PEER_KEYS, tm), lambda i: (0, 0, i))
    return pl.pallas_call(
        _route_kernel,
        grid=(n // tm,),
        in_specs=[pl.BlockSpec((tm, d), lambda i: (i, 0)),
                  pl.BlockSpec(wq_b.shape, lambda i: (0, 0)),
                  pl.BlockSpec(keys.shape, lambda i: (0, 0, 0, 0))],
        out_specs=[ospec] * 4,
        out_shape=[out] * 4,
        compiler_params=_params("arbitrary"),
        name="route",
    )(hb, wq_b, keys)


def _expert_kernel(hb_ref, u_ref, vt_ref, c_ref, bs_ref, wa_ref, eb_ref, o_ref, acc_ref, xu_ref, wgt_ref):
    e = pl.program_id(1)
    nblocks = pl.num_programs(1) - 2
    groups = u_ref.shape[0] // PEER_KEYS
    cur = e % 2
    prev = 1 - cur

    @pl.when(e == 0)
    def _():
        acc_ref[...] = jnp.zeros_like(acc_ref)
        xu_ref[1] = jnp.zeros(xu_ref.shape[1:], F32)
        wgt_ref[1] = jnp.zeros(wgt_ref.shape[1:], BF16)

    acc_ref[...] += jnp.dot(vt_ref[...], wgt_ref[prev], preferred_element_type=F32)

    block = jnp.clip(e - 1, 0, nblocks - 1)
    xu = xu_ref[prev]
    act = 0.5 * xu * (1.0 + lax.erf(xu * (1.0 / math.sqrt(2.0))))
    for r in range(groups):
        a = block * groups + r
        g = None
        for h in range(PEER_HEADS):
            c = c_ref[h, pl.ds(a, 1), :]
            w = wa_ref[h, pl.ds(a, 1), :]
            term = jnp.where(bs_ref[h] >= c, eb_ref[h], 0.0) * w
            g = term if g is None else g + term
        rows = slice(r * PEER_KEYS, (r + 1) * PEER_KEYS)
        wgt_ref[cur, rows, :] = (g * act[rows]).astype(BF16)

    xu_ref[cur] = _nt_dot(u_ref[...], hb_ref[...])

    @pl.when(e == pl.num_programs(1) - 1)
    def _():
        o_ref[...] = acc_ref[...].T


EXPERT_BLOCK = 512


def _expert_blocks_t(v):
    ne, d = v.shape
    te = _tile(ne, EXPERT_BLOCK)
    return v.reshape(ne // te, te, d).transpose(0, 2, 1)


def _expert(hb, u_b, vt_b, c, bs, wa, eb):
    n, d = hb.shape
    nblocks, _, te = vt_b.shape
    tm = _tile(n, 512)
    rspec = pl.BlockSpec((PEER_HEADS, PEER_KEYS, tm), lambda i, e: (0, 0, i))
    return pl.pallas_call(
        _expert_kernel,
        grid=(n // tm, nblocks + 2),
        in_specs=[pl.BlockSpec((tm, d), lambda i, e: (i, 0)),
                  pl.BlockSpec((te, d), lambda i, e: (jnp.minimum(e, nblocks - 1), 0)),
                  pl.BlockSpec((None, d, te), lambda i, e: (jnp.maximum(e - 2, 0), 0, 0)),
                  rspec, rspec, rspec, rspec],
        out_specs=pl.BlockSpec((tm, d), lambda i, e: (i, 0)),
        out_shape=jax.ShapeDtypeStruct((n, d), F32),
        scratch_shapes=[pltpu.VMEM((d, tm), F32), pltpu.VMEM((2, te, tm), F32), pltpu.VMEM((2, te, tm), BF16)],
        compiler_params=_params("arbitrary", "arbitrary"),
        name="expert",
    )(hb, u_b, vt_b, c, bs, wa, eb)


def _final_kernel(alpha, h_ref, peer_ref, p_ref, gw_ref, gb_ref, pw_ref, g_ref, b_ref, y_ref):
    h2 = _layer_norm(alpha * h_ref[...] + peer_ref[...], g_ref[...], b_ref[...])
    gate = jax.nn.sigmoid(jnp.dot(h2.astype(BF16), gw_ref[...], preferred_element_type=F32) + gb_ref[...])
    emb = jnp.dot(p_ref[...].astype(BF16), pw_ref[...], preferred_element_type=F32)
    y_ref[...] = h2 + gate * emb


def _final(alpha, h, peer_out, p, gate_w_b, gate_b, ple_w_b, ln_g, ln_b):
    n, d = h.shape
    pd = p.shape[1]
    tm = _tile(n, 256)
    row = lambda i: (i, 0)
    full = lambda i: (0, 0)
    return pl.pallas_call(
        functools.partial(_final_kernel, alpha),
        grid=(n // tm,),
        in_specs=[pl.BlockSpec((tm, d), row), pl.BlockSpec((tm, d), row), pl.BlockSpec((tm, pd), row),
                  pl.BlockSpec((d, d), full), pl.BlockSpec((1, d), full), pl.BlockSpec((pd, d), full),
                  pl.BlockSpec((1, d), full), pl.BlockSpec((1, d), full)],
        out_specs=pl.BlockSpec((tm, d), row),
        out_shape=jax.ShapeDtypeStruct((n, d), F32),
        compiler_params=_params("arbitrary"),
        name="final",
    )(h, peer_out, p, gate_w_b, gate_b, ple_w_b, ln_g, ln_b)


def _channel_mix(alpha, x, p, attn, conv, lw):
    h, hb = _outproj(alpha, attn, conv, x, lw["w_out"], lw["ln1_g"], lw["ln1_b"])
    c, bs, wa, eb = _route(hb, lw["peer_wq"], lw["peer_keys"])
    peer_out = _expert(hb, lw["peer_u"], lw["peer_vt"], c, bs, wa, eb)
    return _final(alpha, h, peer_out, p, lw["gate_w"], lw["gate_b"], lw["ple_w"], lw["ln2_g"], lw["ln2_b"])


def kernel(x_prompt, x_sample, cache_k, cache_v, state_conv, page_table, p_prompt, p_sample, w_in, w_out, conv_w, conv_b, conv_ln_g, conv_ln_b, ln1_g, ln1_b, peer_wq, peer_keys, peer_u, peer_v, ln2_g, ln2_b, ple_w, gate_w, gate_b):
    depth = w_in.shape[0]
    batch, seq, d = x_prompt.shape
    nseq, dec_seq, _ = x_sample.shape
    c_conv = conv_w.shape[2]
    past_len = page_table.shape[1] * PAGE_SIZE
    alpha = (2.0 * depth) ** 0.25
    assert batch == 1 and w_in.shape[2] == 3 * ATTN_W + 2 * c_conv and c_conv == ATTN_W
    assert peer_keys.shape[1:] == (PEER_HEADS, 2, PEER_KEYS, PEER_HALF)

    cos_p, sin_p = _rotary_tables(jnp.arange(seq, dtype=jnp.int32))
    cos_s, sin_s = _rotary_tables(past_len + jnp.arange(dec_seq, dtype=jnp.int32))
    cos_s = jnp.tile(cos_s, (nseq, 1))
    sin_s = jnp.tile(sin_s, (nseq, 1))

    xp = x_prompt.reshape(seq, d)
    xs = x_sample.reshape(nseq * dec_seq, d)
    outs = [[] for _ in range(6)]
    for i in range(depth):
        vec = lambda a: a[i].reshape(1, -1)
        lw = dict(w_out=w_out[i].astype(BF16), ln1_g=vec(ln1_g), ln1_b=vec(ln1_b),
                  peer_wq=peer_wq[i].astype(BF16), peer_keys=peer_keys[i],
                  peer_u=peer_u[i].astype(BF16), peer_vt=_expert_blocks_t(peer_v[i].astype(BF16)),
                  gate_w=gate_w[i].astype(BF16), gate_b=vec(gate_b), ple_w=ple_w[i].astype(BF16),
                  ln2_g=vec(ln2_g), ln2_b=vec(ln2_b))
        cw = (conv_w[i], vec(conv_b), vec(conv_ln_g), vec(conv_ln_b))
        w_in_b = w_in[i].astype(BF16)

        qp, kp, vp, up = _proj(xp, w_in_b, cos_p, sin_p)
        attn_p = _attn_prompt(qp, kp, vp)
        conv_p = _conv_prompt(up, *cw)
        xp_next = _channel_mix(alpha, xp, p_prompt[i].reshape(seq, -1), attn_p, conv_p, lw)

        qs, ks, vs, us = _proj(xs, w_in_b, cos_s, sin_s)
        attn_s = _attn_sample(i, qs, ks, vs, cache_k, cache_v, page_table)
        hist_s = jnp.concatenate([state_conv[i], us.reshape(nseq, dec_seq, c_conv)], axis=1)
        conv_s = _conv_sample(hist_s, *cw).reshape(nseq * dec_seq, c_conv)
        xs = _channel_mix(alpha, xs, p_sample[i].reshape(nseq * dec_seq, -1), attn_s, conv_s, lw)
        xp = xp_next

        hist_tail = jnp.concatenate([jnp.zeros((CONV_W - 1, c_conv), F32), up], axis=0)[-(CONV_W - 1):]
        outs[0].append(kp.reshape(batch, seq, N_HEADS, HEAD_DIM))
        outs[1].append(vp.reshape(batch, seq, N_HEADS, HEAD_DIM))
        outs[2].append(hist_tail.reshape(batch, CONV_W - 1, c_conv))
        outs[3].append(ks.reshape(nseq, dec_seq, N_HEADS, HEAD_DIM))
        outs[4].append(vs.reshape(nseq, dec_seq, N_HEADS, HEAD_DIM))
        outs[5].append(hist_s[:, -(CONV_W - 1):])
    kp_l, vp_l, cp_l, ks_l, vs_l, cs_l = [jnp.stack(o) for o in outs]
    return (xp.reshape(batch, seq, d), xs.reshape(nseq, dec_seq, d), kp_l, vp_l, cp_l, ks_l, vs_l, cs_l)
```

```python
import functools
import math

import jax
import jax.numpy as jnp
from jax import lax
from jax.experimental import pallas as pl
from jax.experimental.pallas import tpu as pltpu

N_HEADS = 8
HEAD_DIM = 128
ATTN_W = N_HEADS * HEAD_DIM
CONV_W = 31
MOBA_BLOCK = 256
MOBA_TOP_K = 3
ROPE_THETA = 10000.0
PAGE_SIZE = 128
PEER_HEADS = 8
PEER_KEYS = 128
PEER_TOPK = 16
PEER_HALF = 128
LN_EPS = 1e-5

LANES = 128
SUBLANES = 8
MXU_DEPTH = 256
VMEM_LIMIT = 56 * 1024 * 1024

NEG = -1e30
F32 = jnp.float32
BF16 = jnp.bfloat16
NT_DIMS = (((1,), (1,)), ((), ()))


def _nt_dot(a, b, precision=None):
    return lax.dot_general(a, b, NT_DIMS, precision=precision, preferred_element_type=F32)


def _layer_norm(x, g, b):
    mu = jnp.mean(x, axis=-1, keepdims=True)
    xc = x - mu
    var = jnp.mean(xc * xc, axis=-1, keepdims=True)
    return xc * lax.rsqrt(var + LN_EPS) * g + b


def _params(*sem):
    return pltpu.CompilerParams(dimension_semantics=sem, vmem_limit_bytes=VMEM_LIMIT)


def _tile(n, t):
    t = min(n, t)
    assert n % t == 0, (n, t)
    return t


def _proj_kernel(x_ref, w_ref, cos_ref, sin_ref, q_ref, k_ref, v_ref, u_ref, xb_ref, a_ref):
    j = pl.program_id(1)

    @pl.when(j == 0)
    def _():
        xb_ref[...] = x_ref[...].astype(BF16)

    z = jnp.dot(xb_ref[...], w_ref[...], preferred_element_type=F32)

    def rotary_to(o_ref):
        c = cos_ref[...]
        s = sin_ref[...]
        for h in range(N_HEADS):
            zh = z[:, h * HEAD_DIM:(h + 1) * HEAD_DIM]
            o_ref[:, h * HEAD_DIM:(h + 1) * HEAD_DIM] = zh * c + pltpu.roll(zh, HEAD_DIM // 2, 1) * s

    @pl.when(j == 0)
    def _():
        rotary_to(q_ref)

    @pl.when(j == 1)
    def _():
        rotary_to(k_ref)

    @pl.when(j == 2)
    def _():
        v_ref[...] = z

    @pl.when(j == 3)
    def _():
        a_ref[...] = z

    @pl.when(j == 4)
    def _():
        u_ref[...] = a_ref[...] * jax.nn.sigmoid(z)


def _proj(x, w_in_b, cos_t, sin_t):
    n, d = x.shape
    tm = _tile(n, 512)
    row = lambda i, j: (i, 0)
    out = jax.ShapeDtypeStruct((n, ATTN_W), F32)
    return pl.pallas_call(
        _proj_kernel,
        grid=(n // tm, 5),
        in_specs=[pl.BlockSpec((tm, d), row),
                  pl.BlockSpec((d, ATTN_W), lambda i, j: (0, j)),
                  pl.BlockSpec((tm, HEAD_DIM), row),
                  pl.BlockSpec((tm, HEAD_DIM), row)],
        out_specs=[pl.BlockSpec((tm, ATTN_W), row)] * 4,
        out_shape=[out] * 4,
        scratch_shapes=[pltpu.VMEM((tm, d), BF16), pltpu.VMEM((tm, ATTN_W), F32)],
        compiler_params=_params("arbitrary", "arbitrary"),
        name="proj",
    )(x, w_in_b, cos_t, sin_t)


def _rotary_tables(pos):
    half = HEAD_DIM // 2
    inv = ROPE_THETA ** (-jnp.arange(half, dtype=F32) * 2.0 / HEAD_DIM)
    ang = pos.astype(F32)[:, None] * inv[None, :]
    c, s = jnp.cos(ang), jnp.sin(ang)
    return jnp.concatenate([c, c], axis=1), jnp.concatenate([-s, s], axis=1)


EXP2_SCALE = HEAD_DIM ** -0.5 * math.log2(math.e)


def _attn_prompt_kernel(grp, q_ref, k_ref, v_ref, o_ref,
                        kaug_ref, vt_ref, kmean_ref, qaug_ref, m_ref, l_ref, acc_ref):
    i = pl.program_id(1)
    blk = MOBA_BLOCK
    nb = k_ref.shape[0] // blk
    nbp = -(-nb // SUBLANES) * SUBLANES
    heads = range(kaug_ref.shape[0])
    cols = [slice(h * HEAD_DIM, (h + 1) * HEAD_DIM) for h in heads]

    @pl.when(i == 0)
    def _():
        kmean_ref[...] = jnp.zeros_like(kmean_ref)

        def fill(j, carry):
            r = pl.multiple_of(j * blk, blk)
            hot = (lax.broadcasted_iota(jnp.int32, (blk, LANES), 1) == j).astype(F32).astype(BF16)
            for h in heads:
                kb = k_ref[pl.ds(r, blk), cols[h]]
                kmean_ref[h, pl.ds(j, 1), :] = jnp.sum(kb, axis=0, keepdims=True) * (1.0 / blk)
                kaug_ref[h, pl.ds(r, blk), 0:HEAD_DIM] = kb.astype(BF16)
                kaug_ref[h, pl.ds(r, blk), HEAD_DIM:HEAD_DIM + LANES] = hot
                vt_ref[h, :, pl.ds(r, blk)] = v_ref[pl.ds(r, blk), cols[h]].T.astype(BF16)
            return carry

        lax.fori_loop(0, nb, fill, 0)

    for h in heads:
        q = q_ref[:, cols[h]]
        gate = _nt_dot(kmean_ref[h, 0:nbp, :], q, precision=lax.Precision.HIGHEST)
        row = lax.broadcasted_iota(jnp.int32, gate.shape, 0)
        past = row < i
        work = jnp.where(past, gate, -jnp.inf)
        sel = jnp.zeros(gate.shape, jnp.bool_)
        for _ in range(MOBA_TOP_K):
            mx = jnp.max(work, axis=0, keepdims=True)
            idx = jnp.min(jnp.where(work == mx, row, jnp.int32(1 << 30)), axis=0, keepdims=True)
            pick = row == idx
            sel = jnp.logical_or(sel, pick)
            work = jnp.where(pick, -jnp.inf, work)
        keep = jnp.logical_or(jnp.logical_and(sel, past), row == i)
        pen = jnp.where(keep, 0.0, NEG)
        pen = jnp.concatenate([pen, jnp.zeros((LANES - nbp, blk), F32)], axis=0)
        qaug_ref[h, 0:HEAD_DIM, :] = q.T.astype(BF16)
        qaug_ref[h, HEAD_DIM:HEAD_DIM + LANES, :] = pen.astype(BF16)

    gk = grp * blk
    gi = i // grp

    def scores(g):
        r = pl.multiple_of(g * gk, gk)
        return [jnp.dot(kaug_ref[h, pl.ds(r, gk), :], qaug_ref[h], preferred_element_type=F32)
                for h in heads], r

    ss, r = scores(gi)
    key_pos = lax.broadcasted_iota(jnp.int32, ss[0].shape, 0) + gi * gk
    qry_pos = lax.broadcasted_iota(jnp.int32, ss[0].shape, 1) + i * blk
    causal = key_pos <= qry_pos
    for h in heads:
        s = jnp.where(causal, ss[h], NEG)
        m0 = jnp.max(s, axis=0, keepdims=True)
        p = jnp.exp2((s - m0) * EXP2_SCALE)
        m_ref[h] = m0
        l_ref[h] = jnp.sum(p, axis=0, keepdims=True)
        acc_ref[h] = jnp.dot(vt_ref[h, :, pl.ds(r, gk)], p.astype(BF16), preferred_element_type=F32)

    def body(g, carry):
        ss, r = scores(g)
        for h in heads:
            s = ss[h]
            m_old = m_ref[h]
            m_new = jnp.maximum(m_old, jnp.max(s, axis=0, keepdims=True))
            alpha = jnp.exp2((m_old - m_new) * EXP2_SCALE)
            p = jnp.exp2((s - m_new) * EXP2_SCALE)
            l_ref[h] = alpha * l_ref[h] + jnp.sum(p, axis=0, keepdims=True)
            acc_ref[h] = alpha * acc_ref[h] + jnp.dot(
                vt_ref[h, :, pl.ds(r, gk)], p.astype(BF16), preferred_element_type=F32)
            m_ref[h] = m_new
        return carry

    lax.fori_loop(0, gi, body, 0)
    for h in heads:
        o_ref[:, cols[h]] = (acc_ref[h] / l_ref[h]).T.astype(o_ref.dtype)


ATTN_HEADS_PER_STEP = 2


def _attn_prompt(q, k, v):
    s = q.shape[0]
    blk = MOBA_BLOCK
    nb = s // blk
    assert s % blk == 0 and nb <= LANES
    grp = next(g for g in (4, 2, 1) if nb % g == 0)
    hp = ATTN_HEADS_PER_STEP
    width = hp * HEAD_DIM
    qspec = pl.BlockSpec((blk, width), lambda h, i: (i, h))
    kvspec = pl.BlockSpec((s, width), lambda h, i: (0, h), pipeline_mode=pl.Buffered(1))
    return pl.pallas_call(
        functools.partial(_attn_prompt_kernel, grp),
        grid=(N_HEADS // hp, nb),
        in_specs=[qspec, kvspec, kvspec],
        out_specs=qspec,
        out_shape=jax.ShapeDtypeStruct((s, ATTN_W), BF16),
        scratch_shapes=[pltpu.VMEM((hp, s, HEAD_DIM + LANES), BF16),
                        pltpu.VMEM((hp, HEAD_DIM, s), BF16),
                        pltpu.VMEM((hp, LANES, HEAD_DIM), F32),
                        pltpu.VMEM((hp, HEAD_DIM + LANES, blk), BF16),
                        pltpu.VMEM((hp, 1, blk), F32),
                        pltpu.VMEM((hp, 1, blk), F32),
                        pltpu.VMEM((hp, HEAD_DIM, blk), F32)],
        compiler_params=_params("arbitrary", "arbitrary"),
        name="attn_prompt",
    )(q, k, v)


def _attn_sample_kernel(pt_ref, q_ref, kn_ref, vn_ref, ke_ref, ko_ref, ve_ref, vo_ref, o_ref,
                        g_ref, m_ref, l_ref, acc_ref):
    del pt_ref
    j = pl.program_id(1)
    nblk = pl.num_programs(1)
    t = q_ref.shape[0]
    scale = HEAD_DIM ** -0.5
    heads = range(N_HEADS)
    q = q_ref[...]
    qh = [q[:, h * HEAD_DIM:(h + 1) * HEAD_DIM] for h in heads]

    def attend(k_heads, v_heads, mask):
        s = jnp.concatenate([_nt_dot(qh[h].astype(BF16), k_heads[h].astype(BF16)) for h in heads], axis=0)
        s = s * scale
        if mask is not None:
            s = jnp.where(mask, s, NEG)
        m = jnp.max(s, axis=1, keepdims=True)
        p = jnp.exp(s - m)
        l = jnp.sum(p, axis=1, keepdims=True)
        acc = jnp.concatenate([
            jnp.dot(p[h * t:(h + 1) * t].astype(BF16), v_heads[h].astype(BF16), preferred_element_type=F32)
            for h in heads], axis=0)
        return m, l, acc

    def paged(even_ref, odd_ref):
        return [jnp.concatenate([r[0, 0, pl.ds(h, PAGE_SIZE, stride=N_HEADS), :] for r in (even_ref, odd_ref)],
                                axis=0) for h in heads]

    k_heads = paged(ke_ref, ko_ref)
    g_ref[j] = jnp.concatenate([
        jnp.sum(qh[h] * (jnp.sum(k_heads[h], axis=0, keepdims=True) * (1.0 / MOBA_BLOCK)), axis=1, keepdims=True)
        for h in heads], axis=0)
    m, l, acc = attend(k_heads, paged(ve_ref, vo_ref), None)
    m_ref[j] = m
    l_ref[j] = l
    acc_ref[j] = acc

    @pl.when(j == nblk - 1)
    def _():
        pad = jnp.zeros((PAGE_SIZE - t, HEAD_DIM), F32)
        kn = [jnp.concatenate([kn_ref[:, h * HEAD_DIM:(h + 1) * HEAD_DIM], pad], axis=0) for h in heads]
        vn = [jnp.concatenate([vn_ref[:, h * HEAD_DIM:(h + 1) * HEAD_DIM], pad], axis=0) for h in heads]
        shape = (N_HEADS * t, PAGE_SIZE)
        tok = lax.broadcasted_iota(jnp.int32, shape, 0) % t
        key = lax.broadcasted_iota(jnp.int32, shape, 1)
        m_own, l_own, acc_own = attend(kn, vn, key <= tok)

        n = g_ref.shape[0]
        gs = [g_ref[a] for a in range(n)]
        valid = []
        for a in range(n):
            rank = jnp.zeros(gs[a].shape, jnp.int32)
            for b in range(n):
                if b == a:
                    continue
                ahead = gs[b] > gs[a]
                if b < a:
                    ahead = jnp.logical_or(ahead, gs[b] == gs[a])
                rank = rank + ahead.astype(jnp.int32)
            valid.append(rank < MOBA_TOP_K)
        m_tot = m_own
        for a in range(n):
            m_tot = jnp.maximum(m_tot, jnp.where(valid[a], m_ref[a], NEG))
        w = jnp.exp(m_own - m_tot)
        l_tot = w * l_own
        acc = w * acc_own
        for a in range(n):
            w = jnp.where(valid[a], jnp.exp(m_ref[a] - m_tot), 0.0)
            l_tot = l_tot + w * l_ref[a]
            acc = acc + w * acc_ref[a]
        out = acc / l_tot
        for h in heads:
            o_ref[:, h * HEAD_DIM:(h + 1) * HEAD_DIM] = out[h * t:(h + 1) * t, :]


def _attn_sample(layer, q, k_new, v_new, cache_k, cache_v, page_table):
    nseq, npages = page_table.shape
    t = q.shape[0] // nseq
    ppb = MOBA_BLOCK // PAGE_SIZE
    assert ppb == 2 and npages % ppb == 0 and t % SUBLANES == 0 and t <= PAGE_SIZE
    assert (npages * PAGE_SIZE) // MOBA_BLOCK == (npages * PAGE_SIZE + t - 1) // MOBA_BLOCK
    assert cache_k.shape[2:] == (PAGE_SIZE, N_HEADS, HEAD_DIM)
    nblk = npages // ppb
    pt = page_table.reshape(-1).astype(jnp.int32)
    page_rows = PAGE_SIZE * N_HEADS
    cache_k = cache_k.reshape(cache_k.shape[0], cache_k.shape[1], page_rows, HEAD_DIM)
    cache_v = cache_v.reshape(cache_v.shape[0], cache_v.shape[1], page_rows, HEAD_DIM)

    new = pl.BlockSpec((t, ATTN_W), lambda b, j, pt: (b, 0))

    def page(off):
        return pl.BlockSpec((1, 1, page_rows, HEAD_DIM),
                            lambda b, j, pt: (layer, pt[b * npages + ppb * j + off], 0, 0))

    rows = N_HEADS * t
    return pl.pallas_call(
        _attn_sample_kernel,
        grid_spec=pltpu.PrefetchScalarGridSpec(
            num_scalar_prefetch=1,
            grid=(nseq, nblk),
            in_specs=[new, new, new, page(0), page(1), page(0), page(1)],
            out_specs=new,
            scratch_shapes=[pltpu.VMEM((nblk, rows, 1), F32),
                            pltpu.VMEM((nblk, rows, 1), F32),
                            pltpu.VMEM((nblk, rows, 1), F32),
                            pltpu.VMEM((nblk, rows, HEAD_DIM), F32)]),
        out_shape=jax.ShapeDtypeStruct((nseq * t, ATTN_W), F32),
        compiler_params=_params("arbitrary", "arbitrary"),
        name="attn_sample",
    )(pt, q, k_new, v_new, cache_k, cache_k, cache_v, cache_v)


CONV_HALO = 32
CONV_CHUNK = 32


def _conv_finish(y, b_ref, g_ref, beta_ref):
    y = _layer_norm(y + b_ref[...], g_ref[...], beta_ref[...])
    return y * jax.nn.sigmoid(y)


def _conv_prompt_kernel(u_ref, halo_ref, w_ref, b_ref, g_ref, beta_ref, o_ref, buf_ref, sh_ref):
    i = pl.program_id(0)
    tm = u_ref.shape[0]
    rows = CONV_HALO + tm
    buf_ref[0:CONV_HALO, :] = jnp.where(i == 0, 0.0, halo_ref[...])
    buf_ref[CONV_HALO:rows, :] = u_ref[...]
    for s in range(1, SUBLANES):
        sh_ref[s - 1, 0:rows - SUBLANES, :] = buf_ref[s:s + rows - SUBLANES, :]
    lead = CONV_HALO - (CONV_W - 1)

    def tap(base, k):
        off = lead + k
        s, q = off % SUBLANES, off - off % SUBLANES
        if s == 0:
            return buf_ref[base + q:base + q + CONV_CHUNK, :]
        return sh_ref[s - 1, base + q:base + q + CONV_CHUNK, :]

    for c in range(tm // CONV_CHUNK):
        base = c * CONV_CHUNK
        acc = w_ref[0:1, :] * tap(base, 0)
        for k in range(1, CONV_W):
            acc = acc + w_ref[k:k + 1, :] * tap(base, k)
        o_ref[base:base + CONV_CHUNK, :] = _conv_finish(acc, b_ref, g_ref, beta_ref).astype(o_ref.dtype)


def _conv_prompt(u, conv_w, conv_b, ln_g, ln_b):
    s, c = u.shape
    tm = _tile(s, 256)
    assert tm % CONV_HALO == 0 and tm % CONV_CHUNK == 0
    per = tm // CONV_HALO
    full = lambda i: (0, 0)
    return pl.pallas_call(
        _conv_prompt_kernel,
        grid=(s // tm,),
        in_specs=[pl.BlockSpec((tm, c), lambda i: (i, 0)),
                  pl.BlockSpec((CONV_HALO, c), lambda i: (jnp.maximum(i * per - 1, 0), 0)),
                  pl.BlockSpec((CONV_W, c), full),
                  pl.BlockSpec((1, c), full), pl.BlockSpec((1, c), full), pl.BlockSpec((1, c), full)],
        out_specs=pl.BlockSpec((tm, c), lambda i: (i, 0)),
        out_shape=jax.ShapeDtypeStruct((s, c), BF16),
        scratch_shapes=[pltpu.VMEM((CONV_HALO + tm, c), F32),
                        pltpu.VMEM((SUBLANES - 1, CONV_HALO + tm, c), F32)],
        compiler_params=_params("arbitrary"),
        name="conv_prompt",
    )(u, u, conv_w, conv_b, ln_g, ln_b)


def _conv_sample_kernel(hist_ref, w_ref, b_ref, g_ref, beta_ref, o_ref):
    t = o_ref.shape[1]
    for s in range(hist_ref.shape[0]):
        acc = w_ref[0:1, :] * hist_ref[s, 0:t, :]
        for k in range(1, CONV_W):
            acc = acc + w_ref[k:k + 1, :] * hist_ref[s, k:k + t, :]
        o_ref[s] = _conv_finish(acc, b_ref, g_ref, beta_ref)


def _conv_sample(hist, conv_w, conv_b, ln_g, ln_b):
    nseq, rows, c = hist.shape
    t = rows - (CONV_W - 1)
    gs = _tile(nseq, 8)
    full = lambda i: (0, 0)
    return pl.pallas_call(
        _conv_sample_kernel,
        grid=(nseq // gs,),
        in_specs=[pl.BlockSpec((gs, rows, c), lambda i: (i, 0, 0)),
                  pl.BlockSpec((CONV_W, c), full),
                  pl.BlockSpec((1, c), full), pl.BlockSpec((1, c), full), pl.BlockSpec((1, c), full)],
        out_specs=pl.BlockSpec((gs, t, c), lambda i: (i, 0, 0)),
        out_shape=jax.ShapeDtypeStruct((nseq, t, c), F32),
        compiler_params=_params("arbitrary"),
        name="conv_sample",
    )(hist, conv_w, conv_b, ln_g, ln_b)


def _outproj_kernel(alpha, attn_ref, conv_ref, x_ref, wa_ref, wc_ref, g_ref, b_ref, h_ref, hb_ref):
    mixed = jnp.dot(attn_ref[...].astype(BF16), wa_ref[...], preferred_element_type=F32)
    mixed = mixed + jnp.dot(conv_ref[...].astype(BF16), wc_ref[...], preferred_element_type=F32)
    h = _layer_norm(alpha * x_ref[...] + mixed, g_ref[...], b_ref[...])
    h_ref[...] = h
    hb_ref[...] = h.astype(BF16)


def _outproj(alpha, attn, conv, x, w_out_b, ln_g, ln_b):
    n, d = x.shape
    wa = attn.shape[1]
    wc = conv.shape[1]
    tm = _tile(n, 256)
    row = lambda i: (i, 0)
    full = lambda i: (0, 0)
    return pl.pallas_call(
        functools.partial(_outproj_kernel, alpha),
        grid=(n // tm,),
        in_specs=[pl.BlockSpec((tm, wa), row), pl.BlockSpec((tm, wc), row), pl.BlockSpec((tm, d), row),
                  pl.BlockSpec((wa, d), full), pl.BlockSpec((wc, d), lambda i: (wa // wc, 0)),
                  pl.BlockSpec((1, d), full), pl.BlockSpec((1, d), full)],
        out_specs=[pl.BlockSpec((tm, d), row)] * 2,
        out_shape=[jax.ShapeDtypeStruct((n, d), F32), jax.ShapeDtypeStruct((n, d), BF16)],
        compiler_params=_params("arbitrary"),
        name="outproj",
    )(attn, conv, x, w_out_b, w_out_b, ln_g, ln_b)


def _top_sorted(x, k):
    out = []
    rank = jnp.full(x.shape, float(k), F32)
    for r in range(k):
        mx = jnp.max(x, axis=0, keepdims=True)
        out.append(mx)
        hit = x >= mx
        rank = jnp.where(hit, float(r), rank)
        x = jnp.where(hit, -jnp.inf, x)
    return out, rank


def _route_kernel(hb_ref, wq_ref, keys_ref, cnt_ref, rb_ref, wa_ref, eb_ref):
    q = jnp.dot(hb_ref[...], wq_ref[...], preferred_element_type=F32)
    k = PEER_TOPK
    for h in range(PEER_HEADS):
        lo = h * 2 * PEER_HALF
        sa = _nt_dot(keys_ref[h, 0], q[:, lo:lo + PEER_HALF], precision=lax.Precision.HIGHEST)
        sb = _nt_dot(keys_ref[h, 1], q[:, lo + PEER_HALF:lo + 2 * PEER_HALF],
                     precision=lax.Precision.HIGHEST)
        ta, _ = _top_sorted(sa, k + 1)
        tb_rows, rank_b = _top_sorted(sb, k + 1)
        tb = jnp.concatenate(tb_rows[:k], axis=0)
        half = k // 2
        cand = [ta[0] + tb]
        cand += [ta[i] + tb[0:half] for i in range(1, half)]
        cand += [jnp.concatenate(ta[half:k], axis=0) + tb[0:1]]
        row = lax.broadcasted_iota(jnp.int32, (SUBLANES, sa.shape[1]), 0)
        cand += [jnp.where(row == 0, ta[k] + tb_rows[0],
                           jnp.where(row == 1, ta[0] + tb_rows[k], -jnp.inf))]
        cand = jnp.concatenate(cand, axis=0)
        best, _ = _top_sorted(cand, k + 1)
        tau = 0.5 * (best[k - 1] + best[k])
        top = ta[0] + tb[0:1]
        z = jnp.sum(jnp.where(cand >= tau, jnp.exp(cand - top), 0.0), axis=0, keepdims=True)
        need = tau - sa
        cnt = jnp.zeros(sa.shape, F32)
        for j in range(k):
            cnt = cnt + (tb_rows[j] >= need).astype(F32)
        cnt_ref[h] = cnt
        rb_ref[h] = rank_b.astype(BF16)
        wa_ref[h] = jnp.exp(sa - ta[0]) / z
        eb_ref[h] = jnp.exp(sb - tb[0:1]).astype(BF16)


def _route(hb, wq_b, keys):
    n, d = hb.shape
    tm = _tile(n, 256)
    shape = (PEER_HEADS, PEER_KEYS, n)
    outs = [jax.ShapeDtypeStruct(shape, dt) for dt in (F32, BF16, F32, BF16)]
    ospec = pl.BlockSpec((PEER_HEADS, PEER_KEYS, tm), lambda i: (0, 0, i))
    return pl.pallas_call(
        _route_kernel,
        grid=(n // tm,),
        in_specs=[pl.BlockSpec((tm, d), lambda i: (i, 0)),
                  pl.BlockSpec(wq_b.shape, lambda i: (0, 0)),
                  pl.BlockSpec(keys.shape, lambda i: (0, 0, 0, 0))],
        out_specs=[ospec] * 4,
        out_shape=outs,
        compiler_params=_params("arbitrary"),
        name="route",
    )(hb, wq_b, keys)


def _expert_kernel(hb_ref, u_ref, vt_ref, cnt_ref, rb_ref, wa_ref, eb_ref, o_ref, acc_ref, xu_ref, wgt_ref):
    e = pl.program_id(1)
    nblocks = pl.num_programs(1) - 2
    groups = u_ref.shape[0] // PEER_KEYS
    cur = e % 2
    prev = 1 - cur

    @pl.when(e == 0)
    def _():
        acc_ref[...] = jnp.zeros_like(acc_ref)
        xu_ref[1] = jnp.zeros(xu_ref.shape[1:], F32)
        wgt_ref[1] = jnp.zeros(wgt_ref.shape[1:], BF16)

    acc_ref[...] += jnp.dot(vt_ref[...], wgt_ref[prev], preferred_element_type=F32)

    block = jnp.clip(e - 1, 0, nblocks - 1)
    xu = xu_ref[prev]
    act = (0.5 * xu * (1.0 + lax.erf(xu * (1.0 / math.sqrt(2.0))))).astype(BF16)
    shape = (PEER_KEYS, xu.shape[1])
    for r in range(groups):
        a = block * groups + r
        g = None
        for h in range(PEER_HEADS):
            cnt = jnp.broadcast_to(cnt_ref[h, pl.ds(a, 1), :].astype(BF16), shape)
            w = jnp.broadcast_to(wa_ref[h, pl.ds(a, 1), :].astype(BF16), shape)
            term = jnp.where(rb_ref[h] < cnt, eb_ref[h], jnp.zeros(shape, BF16)) * w
            g = term if g is None else g + term
        rows = slice(r * PEER_KEYS, (r + 1) * PEER_KEYS)
        wgt_ref[cur, rows, :] = g * act[rows]

    xu_ref[cur] = _nt_dot(u_ref[...], hb_ref[...])

    @pl.when(e == pl.num_programs(1) - 1)
    def _():
        o_ref[...] = acc_ref[...].T


EXPERT_BLOCK = 512


def _expert_blocks_t(v):
    ne, d = v.shape
    te = _tile(ne, EXPERT_BLOCK)
    return v.reshape(ne // te, te, d).transpose(0, 2, 1)


def _expert(hb, u_b, vt_b, cnt, rb, wa, eb):
    n, d = hb.shape
    nblocks, _, te = vt_b.shape
    tm = _tile(n, 512)
    rspec = pl.BlockSpec((PEER_HEADS, PEER_KEYS, tm), lambda i, e: (0, 0, i))
    return pl.pallas_call(
        _expert_kernel,
        grid=(n // tm, nblocks + 2),
        in_specs=[pl.BlockSpec((tm, d), lambda i, e: (i, 0)),
                  pl.BlockSpec((te, d), lambda i, e: (jnp.minimum(e, nblocks - 1), 0)),
                  pl.BlockSpec((None, d, te), lambda i, e: (jnp.maximum(e - 2, 0), 0, 0)),
                  rspec, rspec, rspec, rspec],
        out_specs=pl.BlockSpec((tm, d), lambda i, e: (i, 0)),
        out_shape=jax.ShapeDtypeStruct((n, d), F32),
        scratch_shapes=[pltpu.VMEM((d, tm), F32), pltpu.VMEM((2, te, tm), F32), pltpu.VMEM((2, te, tm), BF16)],
        compiler_params=_params("arbitrary", "arbitrary"),
        name="expert",
    )(hb, u_b, vt_b, cnt, rb, wa, eb)


def _final_kernel(alpha, h_ref, peer_ref, p_ref, gw_ref, gb_ref, pw_ref, g_ref, b_ref, y_ref):
    h2 = _layer_norm(alpha * h_ref[...] + peer_ref[...], g_ref[...], b_ref[...])
    gate = jax.nn.sigmoid(jnp.dot(h2.astype(BF16), gw_ref[...], preferred_element_type=F32) + gb_ref[...])
    emb = jnp.dot(p_ref[...].astype(BF16), pw_ref[...], preferred_element_type=F32)
    y_ref[...] = h2 + gate * emb


def _final(alpha, h, peer_out, p, gate_w_b, gate_b, ple_w_b, ln_g, ln_b):
    n, d = h.shape
    pd = p.shape[1]
    tm = _tile(n, 256)
    row = lambda i: (i, 0)
    full = lambda i: (0, 0)
    return pl.pallas_call(
        functools.partial(_final_kernel, alpha),
        grid=(n // tm,),
        in_specs=[pl.BlockSpec((tm, d), row), pl.BlockSpec((tm, d), row), pl.BlockSpec((tm, pd), row),
                  pl.BlockSpec((d, d), full), pl.BlockSpec((1, d), full), pl.BlockSpec((pd, d), full),
                  pl.BlockSpec((1, d), full), pl.BlockSpec((1, d), full)],
        out_specs=pl.BlockSpec((tm, d), row),
        out_shape=jax.ShapeDtypeStruct((n, d), F32),
        compiler_params=_params("arbitrary"),
        name="final",
    )(h, peer_out, p, gate_w_b, gate_b, ple_w_b, ln_g, ln_b)


def _channel_mix(alpha, x, p, attn, conv, lw):
    h, hb = _outproj(alpha, attn, conv, x, lw["w_out"], lw["ln1_g"], lw["ln1_b"])
    cnt, rb, wa, eb = _route(hb, lw["peer_wq"], lw["peer_keys"])
    peer_out = _expert(hb, lw["peer_u"], lw["peer_vt"], cnt, rb, wa, eb)
    return _final(alpha, h, peer_out, p, lw["gate_w"], lw["gate_b"], lw["ple_w"], lw["ln2_g"], lw["ln2_b"])


def kernel(x_prompt, x_sample, cache_k, cache_v, state_conv, page_table, p_prompt, p_sample, w_in, w_out, conv_w, conv_b, conv_ln_g, conv_ln_b, ln1_g, ln1_b, peer_wq, peer_keys, peer_u, peer_v, ln2_g, ln2_b, ple_w, gate_w, gate_b):
    depth = w_in.shape[0]
    batch, seq, d = x_prompt.shape
    nseq, dec_seq, _ = x_sample.shape
    c_conv = conv_w.shape[2]
    past_len = page_table.shape[1] * PAGE_SIZE
    alpha = (2.0 * depth) ** 0.25
    assert batch == 1 and w_in.shape[2] == 3 * ATTN_W + 2 * c_conv and c_conv == ATTN_W
    assert peer_keys.shape[1:] == (PEER_HEADS, 2, PEER_KEYS, PEER_HALF)

    cos_p, sin_p = _rotary_tables(jnp.arange(seq, dtype=jnp.int32))
    cos_s, sin_s = _rotary_tables(past_len + jnp.arange(dec_seq, dtype=jnp.int32))
    cos_s = jnp.tile(cos_s, (nseq, 1))
    sin_s = jnp.tile(sin_s, (nseq, 1))

    xp = x_prompt.reshape(seq, d)
    xs = x_sample.reshape(nseq * dec_seq, d)
    outs = [[] for _ in range(6)]
    for i in range(depth):
        vec = lambda a: a[i].reshape(1, -1)
        lw = dict(w_out=w_out[i].astype(BF16), ln1_g=vec(ln1_g), ln1_b=vec(ln1_b),
                  peer_wq=peer_wq[i].astype(BF16), peer_keys=peer_keys[i],
                  peer_u=peer_u[i].astype(BF16), peer_vt=_expert_blocks_t(peer_v[i].astype(BF16)),
                  gate_w=gate_w[i].astype(BF16), gate_b=vec(gate_b), ple_w=ple_w[i].astype(BF16),
                  ln2_g=vec(ln2_g), ln2_b=vec(ln2_b))
        cw = (conv_w[i], vec(conv_b), vec(conv_ln_g), vec(conv_ln_b))
        w_in_b = w_in[i].astype(BF16)

        qp, kp, vp, up = _proj(xp, w_in_b, cos_p, sin_p)
        attn_p = _attn_prompt(qp, kp, vp)
        conv_p = _conv_prompt(up, *cw)
        xp_next = _channel_mix(alpha, xp, p_prompt[i].reshape(seq, -1), attn_p, conv_p, lw)

        qs, ks, vs, us = _proj(xs, w_in_b, cos_s, sin_s)
        attn_s = _attn_sample(i, qs, ks, vs, cache_k, cache_v, page_table)
        hist_s = jnp.concatenate([state_conv[i], us.reshape(nseq, dec_seq, c_conv)], axis=1)
        conv_s = _conv_sample(hist_s, *cw).reshape(nseq * dec_seq, c_conv)
        xs = _channel_mix(alpha, xs, p_sample[i].reshape(nseq * dec_seq, -1), attn_s, conv_s, lw)
        xp = xp_next

        hist_tail = jnp.concatenate([jnp.zeros((CONV_W - 1, c_conv), F32), up], axis=0)[-(CONV_W - 1):]
        outs[0].append(kp.reshape(batch, seq, N_HEADS, HEAD_DIM))
        outs[1].append(vp.reshape(batch, seq, N_HEADS, HEAD_DIM))
        outs[2].append(hist_tail.reshape(batch, CONV_W - 1, c_conv))
        outs[3].append(ks.reshape(nseq, dec_seq, N_HEADS, HEAD_DIM))
        outs[4].append(vs.reshape(nseq, dec_seq, N_HEADS, HEAD_DIM))
        outs[5].append(hist_s[:, -(CONV_W - 1):])
    kp_l, vp_l, cp_l, ks_l, vs_l, cs_l = [jnp.stack(o) for o in outs]
    return (xp.reshape(batch, seq, d), xs.reshape(nseq, dec_seq, d), kp_l, vp_l, cp_l, ks_l, vs_l, cs_l)
```

```python
import functools
import math

import jax
import jax.numpy as jnp
from jax import lax
from jax.experimental import pallas as pl
from jax.experimental.pallas import tpu as pltpu

N_HEADS = 8
HEAD_DIM = 128
ATTN_W = N_HEADS * HEAD_DIM
CONV_W = 31
MOBA_BLOCK = 256
MOBA_TOP_K = 3
ROPE_THETA = 10000.0
PAGE_SIZE = 128
PEER_HEADS = 8
PEER_KEYS = 128
PEER_TOPK = 16
PEER_HALF = 128
LN_EPS = 1e-5

LANES = 128
SUBLANES = 8
VMEM_LIMIT = 56 * 1024 * 1024

NEG = -1e30
F32 = jnp.float32
BF16 = jnp.bfloat16
NT_DIMS = (((1,), (1,)), ((), ()))


def _nt_dot(a, b, precision=None):
    return lax.dot_general(a, b, NT_DIMS, precision=precision, preferred_element_type=F32)


def _layer_norm(x, g, b):
    mu = jnp.mean(x, axis=-1, keepdims=True)
    xc = x - mu
    var = jnp.mean(xc * xc, axis=-1, keepdims=True)
    return xc * lax.rsqrt(var + LN_EPS) * g + b


def _params(*sem):
    return pltpu.CompilerParams(dimension_semantics=sem, vmem_limit_bytes=VMEM_LIMIT)


def _tile(n, t):
    t = min(n, t)
    assert n % t == 0, (n, t)
    return t


def _proj_kernel(x_ref, w_ref, cos_ref, sin_ref, q_ref, k_ref, v_ref, u_ref, xb_ref, a_ref):
    j = pl.program_id(1)

    @pl.when(j == 0)
    def _():
        xb_ref[...] = x_ref[...].astype(BF16)

    z = jnp.dot(xb_ref[...], w_ref[...], preferred_element_type=F32)

    def rotary_to(o_ref):
        c = cos_ref[...]
        s = sin_ref[...]
        for h in range(N_HEADS):
            zh = z[:, h * HEAD_DIM:(h + 1) * HEAD_DIM]
            o_ref[:, h * HEAD_DIM:(h + 1) * HEAD_DIM] = zh * c + pltpu.roll(zh, HEAD_DIM // 2, 1) * s

    @pl.when(j == 0)
    def _():
        rotary_to(q_ref)

    @pl.when(j == 1)
    def _():
        rotary_to(k_ref)

    @pl.when(j == 2)
    def _():
        v_ref[...] = z

    @pl.when(j == 3)
    def _():
        a_ref[...] = z

    @pl.when(j == 4)
    def _():
        u_ref[...] = a_ref[...] * jax.nn.sigmoid(z)


def _proj(x, w_in_b, cos_t, sin_t):
    n, d = x.shape
    tm = _tile(n, 512)
    row = lambda i, j: (i, 0)
    out = jax.ShapeDtypeStruct((n, ATTN_W), F32)
    return pl.pallas_call(
        _proj_kernel,
        grid=(n // tm, 5),
        in_specs=[pl.BlockSpec((tm, d), row),
                  pl.BlockSpec((d, ATTN_W), lambda i, j: (0, j)),
                  pl.BlockSpec((tm, HEAD_DIM), row),
                  pl.BlockSpec((tm, HEAD_DIM), row)],
        out_specs=[pl.BlockSpec((tm, ATTN_W), row)] * 4,
        out_shape=[out] * 4,
        scratch_shapes=[pltpu.VMEM((tm, d), BF16), pltpu.VMEM((tm, ATTN_W), F32)],
        compiler_params=_params("arbitrary", "arbitrary"),
        name="proj",
    )(x, w_in_b, cos_t, sin_t)


def _rotary_tables(pos):
    half = HEAD_DIM // 2
    inv = ROPE_THETA ** (-jnp.arange(half, dtype=F32) * 2.0 / HEAD_DIM)
    ang = pos.astype(F32)[:, None] * inv[None, :]
    c, s = jnp.cos(ang), jnp.sin(ang)
    return jnp.concatenate([c, c], axis=1), jnp.concatenate([-s, s], axis=1)


EXP2_SCALE = HEAD_DIM ** -0.5 * math.log2(math.e)


def _attn_prompt_kernel(grp, q_ref, k_ref, v_ref, o_ref,
                        kaug_ref, vt_ref, kmean_ref, qaug_ref, m_ref, l_ref, acc_ref):
    i = pl.program_id(1)
    blk = MOBA_BLOCK
    nb = k_ref.shape[0] // blk
    nbp = -(-nb // SUBLANES) * SUBLANES
    heads = range(kaug_ref.shape[0])
    cols = [slice(h * HEAD_DIM, (h + 1) * HEAD_DIM) for h in heads]

    @pl.when(i == 0)
    def _():
        kmean_ref[...] = jnp.zeros_like(kmean_ref)

        def fill(j, carry):
            r = pl.multiple_of(j * blk, blk)
            hot = (lax.broadcasted_iota(jnp.int32, (blk, LANES), 1) == j).astype(F32).astype(BF16)
            for h in heads:
                kb = k_ref[pl.ds(r, blk), cols[h]]
                kmean_ref[h, pl.ds(j, 1), :] = jnp.sum(kb, axis=0, keepdims=True) * (1.0 / blk)
                kaug_ref[h, pl.ds(r, blk), 0:HEAD_DIM] = kb.astype(BF16)
                kaug_ref[h, pl.ds(r, blk), HEAD_DIM:HEAD_DIM + LANES] = hot
                vt_ref[h, :, pl.ds(r, blk)] = v_ref[pl.ds(r, blk), cols[h]].T.astype(BF16)
            return carry

        lax.fori_loop(0, nb, fill, 0)

    for h in heads:
        q = q_ref[:, cols[h]]
        gate = _nt_dot(kmean_ref[h, 0:nbp, :], q, precision=lax.Precision.HIGHEST)
        row = lax.broadcasted_iota(jnp.int32, gate.shape, 0)
        past = row < i
        work = jnp.where(past, gate, -jnp.inf)
        sel = jnp.zeros(gate.shape, jnp.bool_)
        for _ in range(MOBA_TOP_K):
            mx = jnp.max(work, axis=0, keepdims=True)
            idx = jnp.min(jnp.where(work == mx, row, jnp.int32(1 << 30)), axis=0, keepdims=True)
            pick = row == idx
            sel = jnp.logical_or(sel, pick)
            work = jnp.where(pick, -jnp.inf, work)
        keep = jnp.logical_or(jnp.logical_and(sel, past), row == i)
        pen = jnp.where(keep, 0.0, NEG)
        pen = jnp.concatenate([pen, jnp.zeros((LANES - nbp, blk), F32)], axis=0)
        qaug_ref[h, 0:HEAD_DIM, :] = q.T.astype(BF16)
        qaug_ref[h, HEAD_DIM:HEAD_DIM + LANES, :] = pen.astype(BF16)

    gk = grp * blk
    gi = i // grp

    def scores(g):
        r = pl.multiple_of(g * gk, gk)
        return [jnp.dot(kaug_ref[h, pl.ds(r, gk), :], qaug_ref[h], preferred_element_type=F32)
                for h in heads], r

    ss, r = scores(gi)
    key_pos = lax.broadcasted_iota(jnp.int32, ss[0].shape, 0) + gi * gk
    qry_pos = lax.broadcasted_iota(jnp.int32, ss[0].shape, 1) + i * blk
    causal = key_pos <= qry_pos
    for h in heads:
        s = jnp.where(causal, ss[h], NEG)
        m0 = jnp.max(s, axis=0, keepdims=True)
        p = jnp.exp2((s - m0) * EXP2_SCALE)
        m_ref[h] = m0
        l_ref[h] = jnp.sum(p, axis=0, keepdims=True)
        acc_ref[h] = jnp.dot(vt_ref[h, :, pl.ds(r, gk)], p.astype(BF16), preferred_element_type=F32)

    def body(g, carry):
        ss, r = scores(g)
        for h in heads:
            s = ss[h]
            m_old = m_ref[h]
            m_new = jnp.maximum(m_old, jnp.max(s, axis=0, keepdims=True))
            alpha = jnp.exp2((m_old - m_new) * EXP2_SCALE)
            p = jnp.exp2((s - m_new) * EXP2_SCALE)
            l_ref[h] = alpha * l_ref[h] + jnp.sum(p, axis=0, keepdims=True)
            acc_ref[h] = alpha * acc_ref[h] + jnp.dot(
                vt_ref[h, :, pl.ds(r, gk)], p.astype(BF16), preferred_element_type=F32)
            m_ref[h] = m_new
        return carry

    lax.fori_loop(0, gi, body, 0)
    for h in heads:
        o_ref[:, cols[h]] = (acc_ref[h] / l_ref[h]).T.astype(o_ref.dtype)


ATTN_HEADS_PER_STEP = 2


def _attn_prompt(q, k, v):
    s = q.shape[0]
    blk = MOBA_BLOCK
    nb = s // blk
    assert s % blk == 0 and nb <= LANES
    grp = next(g for g in (4, 2, 1) if nb % g == 0)
    hp = ATTN_HEADS_PER_STEP
    width = hp * HEAD_DIM
    qspec = pl.BlockSpec((blk, width), lambda h, i: (i, h))
    kvspec = pl.BlockSpec((s, width), lambda h, i: (0, h), pipeline_mode=pl.Buffered(1))
    return pl.pallas_call(
        functools.partial(_attn_prompt_kernel, grp),
        grid=(N_HEADS // hp, nb),
        in_specs=[qspec, kvspec, kvspec],
        out_specs=qspec,
        out_shape=jax.ShapeDtypeStruct((s, ATTN_W), BF16),
        scratch_shapes=[pltpu.VMEM((hp, s, HEAD_DIM + LANES), BF16),
                        pltpu.VMEM((hp, HEAD_DIM, s), BF16),
                        pltpu.VMEM((hp, LANES, HEAD_DIM), F32),
                        pltpu.VMEM((hp, HEAD_DIM + LANES, blk), BF16),
                        pltpu.VMEM((hp, 1, blk), F32),
                        pltpu.VMEM((hp, 1, blk), F32),
                        pltpu.VMEM((hp, HEAD_DIM, blk), F32)],
        compiler_params=_params("arbitrary", "arbitrary"),
        name="attn_prompt",
    )(q, k, v)


def _attn_sample_kernel(bps, pt_ref, q_ref, kn_ref, vn_ref, *refs):
    del pt_ref
    npage = 2 * bps
    k_pages, v_pages = refs[:npage], refs[npage:2 * npage]
    o_ref, g_ref, m_ref, l_ref, acc_ref = refs[2 * npage:]
    j = pl.program_id(1)
    nstep = pl.num_programs(1)
    t = q_ref.shape[0]
    scale = HEAD_DIM ** -0.5
    heads = range(N_HEADS)
    q = q_ref[...]
    qh = [q[:, h * HEAD_DIM:(h + 1) * HEAD_DIM] for h in heads]

    def attend(k_heads, v_heads, mask):
        s = jnp.concatenate([_nt_dot(qh[h].astype(BF16), k_heads[h].astype(BF16)) for h in heads], axis=0)
        s = s * scale
        if mask is not None:
            s = jnp.where(mask, s, NEG)
        m = jnp.max(s, axis=1, keepdims=True)
        p = jnp.exp(s - m)
        l = jnp.sum(p, axis=1, keepdims=True)
        acc = jnp.concatenate([
            jnp.dot(p[h * t:(h + 1) * t].astype(BF16), v_heads[h].astype(BF16), preferred_element_type=F32)
            for h in heads], axis=0)
        return m, l, acc

    def paged(even_ref, odd_ref):
        return [jnp.concatenate([r[0, 0, pl.ds(h, PAGE_SIZE, stride=N_HEADS), :] for r in (even_ref, odd_ref)],
                                axis=0) for h in heads]

    for bi in range(bps):
        blk = j * bps + bi
        k_heads = paged(k_pages[2 * bi], k_pages[2 * bi + 1])
        g_ref[blk] = jnp.concatenate([
            jnp.sum(qh[h] * (jnp.sum(k_heads[h], axis=0, keepdims=True) * (1.0 / MOBA_BLOCK)),
                    axis=1, keepdims=True)
            for h in heads], axis=0)
        m, l, acc = attend(k_heads, paged(v_pages[2 * bi], v_pages[2 * bi + 1]), None)
        m_ref[blk] = m
        l_ref[blk] = l
        acc_ref[blk] = acc

    @pl.when(j == nstep - 1)
    def _():
        pad = jnp.zeros((PAGE_SIZE - t, HEAD_DIM), F32)
        kn = [jnp.concatenate([kn_ref[:, h * HEAD_DIM:(h + 1) * HEAD_DIM], pad], axis=0) for h in heads]
        vn = [jnp.concatenate([vn_ref[:, h * HEAD_DIM:(h + 1) * HEAD_DIM], pad], axis=0) for h in heads]
        shape = (N_HEADS * t, PAGE_SIZE)
        tok = lax.broadcasted_iota(jnp.int32, shape, 0) % t
        key = lax.broadcasted_iota(jnp.int32, shape, 1)
        m_own, l_own, acc_own = attend(kn, vn, key <= tok)

        n = g_ref.shape[0]
        gs = [g_ref[a] for a in range(n)]
        valid = []
        for a in range(n):
            rank = jnp.zeros(gs[a].shape, jnp.int32)
            for b in range(n):
                if b == a:
                    continue
                ahead = gs[b] > gs[a]
                if b < a:
                    ahead = jnp.logical_or(ahead, gs[b] == gs[a])
                rank = rank + ahead.astype(jnp.int32)
            valid.append(rank < MOBA_TOP_K)
        m_tot = m_own
        for a in range(n):
            m_tot = jnp.maximum(m_tot, jnp.where(valid[a], m_ref[a], NEG))
        w = jnp.exp(m_own - m_tot)
        l_tot = w * l_own
        acc = w * acc_own
        for a in range(n):
            w = jnp.where(valid[a], jnp.exp(m_ref[a] - m_tot), 0.0)
            l_tot = l_tot + w * l_ref[a]
            acc = acc + w * acc_ref[a]
        out = acc / l_tot
        for h in heads:
            o_ref[:, h * HEAD_DIM:(h + 1) * HEAD_DIM] = out[h * t:(h + 1) * t, :]


def _attn_sample(layer, q, k_new, v_new, cache_k, cache_v, page_table):
    nseq, npages = page_table.shape
    t = q.shape[0] // nseq
    ppb = MOBA_BLOCK // PAGE_SIZE
    assert ppb == 2 and npages % ppb == 0 and t % SUBLANES == 0 and t <= PAGE_SIZE
    assert (npages * PAGE_SIZE) // MOBA_BLOCK == (npages * PAGE_SIZE + t - 1) // MOBA_BLOCK
    assert cache_k.shape[2:] == (PAGE_SIZE, N_HEADS, HEAD_DIM)
    nblk = npages // ppb
    pt = page_table.reshape(-1).astype(jnp.int32)
    page_rows = PAGE_SIZE * N_HEADS
    cache_k = cache_k.reshape(cache_k.shape[0], cache_k.shape[1], page_rows, HEAD_DIM)
    cache_v = cache_v.reshape(cache_v.shape[0], cache_v.shape[1], page_rows, HEAD_DIM)

    new = pl.BlockSpec((t, ATTN_W), lambda b, j, pt: (b, 0))

    bps = 2 if nblk % 2 == 0 else 1
    pps = ppb * bps

    def page(off):
        return pl.BlockSpec((1, 1, page_rows, HEAD_DIM),
                            lambda b, j, pt: (layer, pt[b * npages + pps * j + off], 0, 0))

    pages = [page(off) for off in range(pps)]
    rows = N_HEADS * t
    return pl.pallas_call(
        functools.partial(_attn_sample_kernel, bps),
        grid_spec=pltpu.PrefetchScalarGridSpec(
            num_scalar_prefetch=1,
            grid=(nseq, nblk // bps),
            in_specs=[new, new, new] + pages + pages,
            out_specs=new,
            scratch_shapes=[pltpu.VMEM((nblk, rows, 1), F32),
                            pltpu.VMEM((nblk, rows, 1), F32),
                            pltpu.VMEM((nblk, rows, 1), F32),
                            pltpu.VMEM((nblk, rows, HEAD_DIM), F32)]),
        out_shape=jax.ShapeDtypeStruct((nseq * t, ATTN_W), F32),
        compiler_params=_params("arbitrary", "arbitrary"),
        name="attn_sample",
    )(pt, q, k_new, v_new, *([cache_k] * pps), *([cache_v] * pps))


CONV_HALO = 32
CONV_CHUNK = 32


def _conv_finish(y, b_ref, g_ref, beta_ref):
    y = _layer_norm(y + b_ref[...], g_ref[...], beta_ref[...])
    return y * jax.nn.sigmoid(y)


def _conv_prompt_kernel(u_ref, halo_ref, w_ref, b_ref, g_ref, beta_ref, o_ref, buf_ref, sh_ref):
    i = pl.program_id(0)
    tm = u_ref.shape[0]
    rows = CONV_HALO + tm
    buf_ref[0:CONV_HALO, :] = jnp.where(i == 0, 0.0, halo_ref[...])
    buf_ref[CONV_HALO:rows, :] = u_ref[...]
    for s in range(1, SUBLANES):
        sh_ref[s - 1, 0:rows - SUBLANES, :] = buf_ref[s:s + rows - SUBLANES, :]
    lead = CONV_HALO - (CONV_W - 1)

    def tap(base, k):
        off = lead + k
        s, q = off % SUBLANES, off - off % SUBLANES
        if s == 0:
            return buf_ref[base + q:base + q + CONV_CHUNK, :]
        return sh_ref[s - 1, base + q:base + q + CONV_CHUNK, :]

    for c in range(tm // CONV_CHUNK):
        base = c * CONV_CHUNK
        acc = w_ref[0:1, :] * tap(base, 0)
        for k in range(1, CONV_W):
            acc = acc + w_ref[k:k + 1, :] * tap(base, k)
        o_ref[base:base + CONV_CHUNK, :] = _conv_finish(acc, b_ref, g_ref, beta_ref).astype(o_ref.dtype)


def _conv_prompt(u, conv_w, conv_b, ln_g, ln_b):
    s, c = u.shape
    tm = _tile(s, 256)
    assert tm % CONV_HALO == 0 and tm % CONV_CHUNK == 0
    per = tm // CONV_HALO
    full = lambda i: (0, 0)
    return pl.pallas_call(
        _conv_prompt_kernel,
        grid=(s // tm,),
        in_specs=[pl.BlockSpec((tm, c), lambda i: (i, 0)),
                  pl.BlockSpec((CONV_HALO, c), lambda i: (jnp.maximum(i * per - 1, 0), 0)),
                  pl.BlockSpec((CONV_W, c), full),
                  pl.BlockSpec((1, c), full), pl.BlockSpec((1, c), full), pl.BlockSpec((1, c), full)],
        out_specs=pl.BlockSpec((tm, c), lambda i: (i, 0)),
        out_shape=jax.ShapeDtypeStruct((s, c), BF16),
        scratch_shapes=[pltpu.VMEM((CONV_HALO + tm, c), F32),
                        pltpu.VMEM((SUBLANES - 1, CONV_HALO + tm, c), F32)],
        compiler_params=_params("arbitrary"),
        name="conv_prompt",
    )(u, u, conv_w, conv_b, ln_g, ln_b)


def _conv_sample_kernel(hist_ref, w_ref, b_ref, g_ref, beta_ref, o_ref):
    t = o_ref.shape[1]
    for s in range(hist_ref.shape[0]):
        acc = w_ref[0:1, :] * hist_ref[s, 0:t, :]
        for k in range(1, CONV_W):
            acc = acc + w_ref[k:k + 1, :] * hist_ref[s, k:k + t, :]
        o_ref[s] = _conv_finish(acc, b_ref, g_ref, beta_ref)


def _conv_sample(hist, conv_w, conv_b, ln_g, ln_b):
    nseq, rows, c = hist.shape
    t = rows - (CONV_W - 1)
    gs = _tile(nseq, 8)
    full = lambda i: (0, 0)
    return pl.pallas_call(
        _conv_sample_kernel,
        grid=(nseq // gs,),
        in_specs=[pl.BlockSpec((gs, rows, c), lambda i: (i, 0, 0)),
                  pl.BlockSpec((CONV_W, c), full),
                  pl.BlockSpec((1, c), full), pl.BlockSpec((1, c), full), pl.BlockSpec((1, c), full)],
        out_specs=pl.BlockSpec((gs, t, c), lambda i: (i, 0, 0)),
        out_shape=jax.ShapeDtypeStruct((nseq, t, c), F32),
        compiler_params=_params("arbitrary"),
        name="conv_sample",
    )(hist, conv_w, conv_b, ln_g, ln_b)


def _outproj_kernel(alpha, attn_ref, conv_ref, x_ref, wa_ref, wc_ref, g_ref, b_ref, h_ref, hb_ref):
    mixed = jnp.dot(attn_ref[...].astype(BF16), wa_ref[...], preferred_element_type=F32)
    mixed = mixed + jnp.dot(conv_ref[...].astype(BF16), wc_ref[...], preferred_element_type=F32)
    h = _layer_norm(alpha * x_ref[...] + mixed, g_ref[...], b_ref[...])
    h_ref[...] = h
    hb_ref[...] = h.astype(BF16)


def _outproj(alpha, attn, conv, x, w_out_b, ln_g, ln_b):
    n, d = x.shape
    wa = attn.shape[1]
    wc = conv.shape[1]
    tm = _tile(n, 256)
    row = lambda i: (i, 0)
    full = lambda i: (0, 0)
    return pl.pallas_call(
        functools.partial(_outproj_kernel, alpha),
        grid=(n // tm,),
        in_specs=[pl.BlockSpec((tm, wa), row), pl.BlockSpec((tm, wc), row), pl.BlockSpec((tm, d), row),
                  pl.BlockSpec((wa, d), full), pl.BlockSpec((wc, d), lambda i: (wa // wc, 0)),
                  pl.BlockSpec((1, d), full), pl.BlockSpec((1, d), full)],
        out_specs=[pl.BlockSpec((tm, d), row)] * 2,
        out_shape=[jax.ShapeDtypeStruct((n, d), F32), jax.ShapeDtypeStruct((n, d), BF16)],
        compiler_params=_params("arbitrary"),
        name="outproj",
    )(attn, conv, x, w_out_b, w_out_b, ln_g, ln_b)


def _top_sorted(x, k):
    out = []
    rank = jnp.full(x.shape, float(k), F32)
    for r in range(k):
        mx = jnp.max(x, axis=0, keepdims=True)
        out.append(mx)
        hit = x >= mx
        rank = jnp.where(hit, float(r), rank)
        x = jnp.where(hit, -jnp.inf, x)
    return out, rank


def _route_kernel(hb_ref, wq_ref, keys_ref, cnt_ref, rb_ref, wa_ref, eb_ref):
    q = jnp.dot(hb_ref[...], wq_ref[...], preferred_element_type=F32)
    k = PEER_TOPK
    for h in range(PEER_HEADS):
        lo = h * 2 * PEER_HALF
        sa = _nt_dot(keys_ref[h, 0], q[:, lo:lo + PEER_HALF], precision=lax.Precision.HIGHEST)
        sb = _nt_dot(keys_ref[h, 1], q[:, lo + PEER_HALF:lo + 2 * PEER_HALF],
                     precision=lax.Precision.HIGHEST)
        ta, _ = _top_sorted(sa, k + 1)
        tb_rows, rank_b = _top_sorted(sb, k + 1)
        tb = jnp.concatenate(tb_rows[:k], axis=0)
        half = k // 2
        cand = [ta[0] + tb]
        cand += [ta[i] + tb[0:half] for i in range(1, half)]
        cand += [jnp.concatenate(ta[half:k], axis=0) + tb[0:1]]
        row = lax.broadcasted_iota(jnp.int32, (SUBLANES, sa.shape[1]), 0)
        cand += [jnp.where(row == 0, ta[k] + tb_rows[0],
                           jnp.where(row == 1, ta[0] + tb_rows[k], -jnp.inf))]
        cand = jnp.concatenate(cand, axis=0)
        best, _ = _top_sorted(cand, k + 1)
        tau = 0.5 * (best[k - 1] + best[k])
        top = ta[0] + tb[0:1]
        z = jnp.sum(jnp.where(cand >= tau, jnp.exp(cand - top), 0.0), axis=0, keepdims=True)
        need = tau - sa
        cnt = jnp.zeros(sa.shape, F32)
        for j in range(k):
            cnt = jnp.where(tb_rows[j] >= need, float(j + 1), cnt)
        cnt_ref[h] = cnt
        rb_ref[h] = rank_b.astype(BF16)
        wa_ref[h] = jnp.exp(sa - ta[0]) / z
        eb_ref[h] = jnp.exp(sb - tb[0:1]).astype(BF16)


def _route(hb, wq_b, keys):
    n, d = hb.shape
    tm = _tile(n, 256)
    shape = (PEER_HEADS, PEER_KEYS, n)
    outs = [jax.ShapeDtypeStruct(shape, dt) for dt in (F32, BF16, F32, BF16)]
    ospec = pl.BlockSpec((PEER_HEADS, PEER_KEYS, tm), lambda i: (0, 0, i))
    return pl.pallas_call(
        _route_kernel,
        grid=(n // tm,),
        in_specs=[pl.BlockSpec((tm, d), lambda i: (i, 0)),
                  pl.BlockSpec(wq_b.shape, lambda i: (0, 0)),
                  pl.BlockSpec(keys.shape, lambda i: (0, 0, 0, 0))],
        out_specs=[ospec] * 4,
        out_shape=outs,
        compiler_params=_params("arbitrary"),
        name="route",
    )(hb, wq_b, keys)


def _expert_kernel(hb_ref, u_ref, vt_ref, cnt_ref, rb_ref, wa_ref, eb_ref, o_ref, acc_ref, xu_ref, wgt_ref):
    e = pl.program_id(1)
    nblocks = pl.num_programs(1) - 2
    groups = u_ref.shape[0] // PEER_KEYS
    cur = e % 2
    prev = 1 - cur

    @pl.when(e == 0)
    def _():
        acc_ref[...] = jnp.zeros_like(acc_ref)
        xu_ref[1] = jnp.zeros(xu_ref.shape[1:], F32)
        wgt_ref[1] = jnp.zeros(wgt_ref.shape[1:], BF16)

    acc_ref[...] += jnp.dot(vt_ref[...], wgt_ref[prev], preferred_element_type=F32)

    block = jnp.clip(e - 1, 0, nblocks - 1)
    xu = xu_ref[prev]
    act = (0.5 * xu * (1.0 + lax.erf(xu * (1.0 / math.sqrt(2.0))))).astype(BF16)
    shape = (PEER_KEYS, xu.shape[1])
    for r in range(groups):
        a = block * groups + r
        g = None
        for h in range(PEER_HEADS):
            cnt = jnp.broadcast_to(cnt_ref[h, pl.ds(a, 1), :].astype(BF16), shape)
            w = jnp.broadcast_to(wa_ref[h, pl.ds(a, 1), :].astype(BF16), shape)
            term = jnp.where(rb_ref[h] < cnt, eb_ref[h], jnp.zeros(shape, BF16)) * w
            g = term if g is None else g + term
        rows = slice(r * PEER_KEYS, (r + 1) * PEER_KEYS)
        wgt_ref[cur, rows, :] = g * act[rows]

    xu_ref[cur] = _nt_dot(u_ref[...], hb_ref[...])

    @pl.when(e == pl.num_programs(1) - 1)
    def _():
        o_ref[...] = acc_ref[...].T


EXPERT_BLOCK = 512


def _expert_blocks_t(v):
    ne, d = v.shape
    te = _tile(ne, EXPERT_BLOCK)
    return v.reshape(ne // te, te, d).transpose(0, 2, 1)


def _expert(hb, u_b, vt_b, cnt, rb, wa, eb):
    n, d = hb.shape
    nblocks, _, te = vt_b.shape
    tm = _tile(n, 512)
    rspec = pl.BlockSpec((PEER_HEADS, PEER_KEYS, tm), lambda i, e: (0, 0, i))
    return pl.pallas_call(
        _expert_kernel,
        grid=(n // tm, nblocks + 2),
        in_specs=[pl.BlockSpec((tm, d), lambda i, e: (i, 0)),
                  pl.BlockSpec((te, d), lambda i, e: (jnp.minimum(e, nblocks - 1), 0)),
                  pl.BlockSpec((None, d, te), lambda i, e: (jnp.maximum(e - 2, 0), 0, 0)),
                  rspec, rspec, rspec, rspec],
        out_specs=pl.BlockSpec((tm, d), lambda i, e: (i, 0)),
        out_shape=jax.ShapeDtypeStruct((n, d), F32),
        scratch_shapes=[pltpu.VMEM((d, tm), F32), pltpu.VMEM((2, te, tm), F32), pltpu.VMEM((2, te, tm), BF16)],
        compiler_params=_params("arbitrary", "arbitrary"),
        name="expert",
    )(hb, u_b, vt_b, cnt, rb, wa, eb)


def _final_kernel(alpha, h_ref, peer_ref, p_ref, gw_ref, gb_ref, pw_ref, g_ref, b_ref, y_ref):
    h2 = _layer_norm(alpha * h_ref[...] + peer_ref[...], g_ref[...], b_ref[...])
    gate = jax.nn.sigmoid(jnp.dot(h2.astype(BF16), gw_ref[...], preferred_element_type=F32) + gb_ref[...])
    emb = jnp.dot(p_ref[...].astype(BF16), pw_ref[...], preferred_element_type=F32)
    y_ref[...] = h2 + gate * emb


def _final(alpha, h, peer_out, p, gate_w_b, gate_b, ple_w_b, ln_g, ln_b):
    n, d = h.shape
    pd = p.shape[1]
    tm = _tile(n, 256)
    row = lambda i: (i, 0)
    full = lambda i: (0, 0)
    return pl.pallas_call(
        functools.partial(_final_kernel, alpha),
        grid=(n // tm,),
        in_specs=[pl.BlockSpec((tm, d), row), pl.BlockSpec((tm, d), row), pl.BlockSpec((tm, pd), row),
                  pl.BlockSpec((d, d), full), pl.BlockSpec((1, d), full), pl.BlockSpec((pd, d), full),
                  pl.BlockSpec((1, d), full), pl.BlockSpec((1, d), full)],
        out_specs=pl.BlockSpec((tm, d), row),
        out_shape=jax.ShapeDtypeStruct((n, d), F32),
        compiler_params=_params("arbitrary"),
        name="final",
    )(h, peer_out, p, gate_w_b, gate_b, ple_w_b, ln_g, ln_b)


def _channel_mix(alpha, x, p, attn, conv, lw):
    h, hb = _outproj(alpha, attn, conv, x, lw["w_out"], lw["ln1_g"], lw["ln1_b"])
    cnt, rb, wa, eb = _route(hb, lw["peer_wq"], lw["peer_keys"])
    peer_out = _expert(hb, lw["peer_u"], lw["peer_vt"], cnt, rb, wa, eb)
    return _final(alpha, h, peer_out, p, lw["gate_w"], lw["gate_b"], lw["ple_w"], lw["ln2_g"], lw["ln2_b"])


def kernel(x_prompt, x_sample, cache_k, cache_v, state_conv, page_table, p_prompt, p_sample, w_in, w_out, conv_w, conv_b, conv_ln_g, conv_ln_b, ln1_g, ln1_b, peer_wq, peer_keys, peer_u, peer_v, ln2_g, ln2_b, ple_w, gate_w, gate_b):
    depth = w_in.shape[0]
    batch, seq, d = x_prompt.shape
    nseq, dec_seq, _ = x_sample.shape
    c_conv = conv_w.shape[2]
    past_len = page_table.shape[1] * PAGE_SIZE
    alpha = (2.0 * depth) ** 0.25
    assert batch == 1 and w_in.shape[2] == 3 * ATTN_W + 2 * c_conv and c_conv == ATTN_W
    assert peer_keys.shape[1:] == (PEER_HEADS, 2, PEER_KEYS, PEER_HALF)

    cos_p, sin_p = _rotary_tables(jnp.arange(seq, dtype=jnp.int32))
    cos_s, sin_s = _rotary_tables(past_len + jnp.arange(dec_seq, dtype=jnp.int32))
    cos_s = jnp.tile(cos_s, (nseq, 1))
    sin_s = jnp.tile(sin_s, (nseq, 1))

    xp = x_prompt.reshape(seq, d)
    xs = x_sample.reshape(nseq * dec_seq, d)
    outs = [[] for _ in range(6)]
    for i in range(depth):
        vec = lambda a: a[i].reshape(1, -1)
        lw = dict(w_out=w_out[i].astype(BF16), ln1_g=vec(ln1_g), ln1_b=vec(ln1_b),
                  peer_wq=peer_wq[i].astype(BF16), peer_keys=peer_keys[i],
                  peer_u=peer_u[i].astype(BF16), peer_vt=_expert_blocks_t(peer_v[i].astype(BF16)),
                  gate_w=gate_w[i].astype(BF16), gate_b=vec(gate_b), ple_w=ple_w[i].astype(BF16),
                  ln2_g=vec(ln2_g), ln2_b=vec(ln2_b))
        cw = (conv_w[i], vec(conv_b), vec(conv_ln_g), vec(conv_ln_b))
        w_in_b = w_in[i].astype(BF16)

        qp, kp, vp, up = _proj(xp, w_in_b, cos_p, sin_p)
        attn_p = _attn_prompt(qp, kp, vp)
        conv_p = _conv_prompt(up, *cw)
        xp_next = _channel_mix(alpha, xp, p_prompt[i].reshape(seq, -1), attn_p, conv_p, lw)

        qs, ks, vs, us = _proj(xs, w_in_b, cos_s, sin_s)
        attn_s = _attn_sample(i, qs, ks, vs, cache_k, cache_v, page_table)
        hist_s = jnp.concatenate([state_conv[i], us.reshape(nseq, dec_seq, c_conv)], axis=1)
        conv_s = _conv_sample(hist_s, *cw).reshape(nseq * dec_seq, c_conv)
        xs = _channel_mix(alpha, xs, p_sample[i].reshape(nseq * dec_seq, -1), attn_s, conv_s, lw)
        xp = xp_next

        hist_tail = jnp.concatenate([jnp.zeros((CONV_W - 1, c_conv), F32), up], axis=0)[-(CONV_W - 1):]
        outs[0].append(kp.reshape(batch, seq, N_HEADS, HEAD_DIM))
        outs[1].append(vp.reshape(batch, seq, N_HEADS, HEAD_DIM))
        outs[2].append(hist_tail.reshape(batch, CONV_W - 1, c_conv))
        outs[3].append(ks.reshape(nseq, dec_seq, N_HEADS, HEAD_DIM))
        outs[4].append(vs.reshape(nseq, dec_seq, N_HEADS, HEAD_DIM))
        outs[5].append(hist_s[:, -(CONV_W - 1):])
    kp_l, vp_l, cp_l, ks_l, vs_l, cs_l = [jnp.stack(o) for o in outs]
    return (xp.reshape(batch, seq, d), xs.reshape(nseq, dec_seq, d), kp_l, vp_l, cp_l, ks_l, vs_l, cs_l)
```
